```python
import jax, jax.numpy as jnp
from jax import lax
import numpy as np

D_MODEL = 1024
BATCH = 2
SEQ = 8192
DEPTH = 4
DEC_BATCH = 128
DEC_SEQ = 1
PAST_LEN = 8192
PAGE_SIZE = 128

HEAD_DIM = 64
N_A_LAYERS = DEPTH // 2
N_B_LAYERS = DEPTH - N_A_LAYERS
A_HEADS = D_MODEL // HEAD_DIM
A_KV_HEADS = 4
A_GROUP = A_HEADS // A_KV_HEADS
B_HEADS = D_MODEL // HEAD_DIM
B_KV_HEADS = 4
B_GROUP = B_HEADS // B_KV_HEADS
BRANCH = A_HEADS * HEAD_DIM
A_KV_W = A_KV_HEADS * HEAD_DIM
B_KV_W = B_KV_HEADS * HEAD_DIM
WINDOW = 128
BLOCK = 128
ROPE_THETA = 10000.0
LN_EPS = 1e-5
DEEPNORM_ALPHA = (2.0 * DEPTH) ** 0.25
DEEPNORM_BETA = (8.0 * DEPTH) ** -0.25
SB_BIAS_INIT = -7.0
NEG_INF = -1e30

kernel_name = "yoco_stickbreak_swa_sink_decoder_step"


def layer_norm(x, g, b):
    xf = x.astype(jnp.float32)
    mu = jnp.mean(xf, axis=-1, keepdims=True)
    var = jnp.mean(jnp.square(xf - mu), axis=-1, keepdims=True)
    return ((xf - mu) * lax.rsqrt(var + LN_EPS) * g.astype(jnp.float32)
            + b.astype(jnp.float32)).astype(x.dtype)


def rope(x, pos):
    d = x.shape[-1]
    half = d // 2
    inv = ROPE_THETA ** (-jnp.arange(half, dtype=jnp.float32) * 2.0 / d)
    ang = pos.astype(jnp.float32)[:, None] * inv[None, :]
    shape = (1, pos.shape[0]) + (1,) * (x.ndim - 3) + (half,)
    cos = jnp.cos(ang).reshape(shape)
    sin = jnp.sin(ang).reshape(shape)
    xf = x.astype(jnp.float32)
    x1, x2 = xf[..., :half], xf[..., half:]
    return jnp.concatenate([x1 * cos - x2 * sin, x2 * cos + x1 * sin], axis=-1).astype(x.dtype)


def gather_pages(pool, page_table):
    rows = pool[page_table]
    return rows.reshape((page_table.shape[0], -1) + pool.shape[2:])


def stick_breaking_weights(z, q_pos, k_pos):
    mask = k_pos[None, :] < q_pos[:, None]
    log_keep = jnp.where(mask, jax.nn.log_sigmoid(-z), 0.0)
    between = lax.cumsum(log_keep, axis=z.ndim - 1, reverse=True) - log_keep
    return jnp.where(mask, jnp.exp(jax.nn.log_sigmoid(z) + between), 0.0)


def stick_breaking_prompt(q, k, v, bias):
    b, s, g, r, d = q.shape
    nb = s // BLOCK
    qb = q.reshape(b, nb, BLOCK, g, r, d).transpose(1, 0, 2, 3, 4, 5)
    k_pos = jnp.arange(s)
    scale = d ** -0.5
    bz = bias[None, :, :, None, None]

    def one_block(args):
        q_blk, start = args
        z = jnp.einsum('btgrd,bsgd->bgrts', q_blk, k,
                       preferred_element_type=jnp.float32) * scale + bz
        a = stick_breaking_weights(z, start + jnp.arange(BLOCK), k_pos)
        return jnp.einsum('bgrts,bsgd->btgrd', a.astype(v.dtype), v)

    out = lax.map(one_block, (qb, jnp.arange(nb) * BLOCK))
    return out.transpose(1, 0, 2, 3, 4, 5).reshape(b, s, g * r * d)


def stick_breaking_sample(q, k_new, v_new, k_past, v_past, bias):
    b, t, g, r, d = q.shape
    past = k_past.shape[1]
    scale = d ** -0.5
    z = jnp.concatenate([
        jnp.einsum('btgrd,bsgd->bgrts', q, k_past, preferred_element_type=jnp.float32),
        jnp.einsum('btgrd,bsgd->bgrts', q, k_new, preferred_element_type=jnp.float32),
    ], axis=-1) * scale + bias[None, :, :, None, None]
    a = stick_breaking_weights(z, past + jnp.arange(t), jnp.arange(past + t)).astype(v_new.dtype)
    out = (jnp.einsum('bgrts,bsgd->btgrd', a[..., :past], v_past)
           + jnp.einsum('bgrts,bsgd->btgrd', a[..., past:], v_new))
    return out.reshape(b, t, g * r * d)


def sink_softmax(s, mask, sink):
    s = jnp.where(mask, s, NEG_INF)
    m = jnp.maximum(jnp.max(s, axis=-1, keepdims=True), sink)
    p = jnp.exp(s - m)
    return p / (jnp.sum(p, axis=-1, keepdims=True) + jnp.exp(sink - m))


def swa_prompt(q, k, v, sink):
    b, s, g, r, d = q.shape
    nb = s // BLOCK
    qb = q.reshape(b, nb, BLOCK, g, r, d)

    def band(x):
        xb = x.reshape(b, nb, BLOCK, g, d)
        prev = jnp.pad(xb, ((0, 0), (1, 0), (0, 0), (0, 0), (0, 0)))[:, :nb]
        return jnp.concatenate([prev, xb], axis=2)

    kb, vb = band(k), band(v)
    sc = jnp.einsum('bntgrd,bnsgd->bngrts', qb, kb,
                    preferred_element_type=jnp.float32) * (d ** -0.5)
    starts = jnp.arange(nb)[:, None] * BLOCK
    q_pos = starts + jnp.arange(BLOCK)[None, :]
    k_pos = starts - BLOCK + jnp.arange(2 * BLOCK)[None, :]
    diff = q_pos[:, :, None] - k_pos[:, None, :]
    mask = (diff >= 0) & (diff <= WINDOW) & (k_pos[:, None, :] >= 0)
    w = sink_softmax(sc, mask[None, :, None, None], sink[None, None, :, :, None, None])
    out = jnp.einsum('bngrts,bnsgd->bntgrd', w.astype(v.dtype), vb)
    return out.reshape(b, s, g * r * d)


def swa_sample(q, k_all, v_all, sink, q_pos, k_pos):
    b, t, g, r, d = q.shape
    sc = jnp.einsum('btgrd,bsgd->bgrts', q, k_all,
                    preferred_element_type=jnp.float32) * (d ** -0.5)
    diff = q_pos[:, None] - k_pos[None, :]
    mask = (diff >= 0) & (diff <= WINDOW)
    w = sink_softmax(sc, mask[None, None, None], sink[None, :, :, None, None])
    out = jnp.einsum('bgrts,bsgd->btgrd', w.astype(v_all.dtype), v_all)
    return out.reshape(b, t, g * r * d)


def split_a(h):
    b, t, _ = h.shape
    q, k, v, gate = jnp.split(h, [BRANCH, BRANCH + A_KV_W, BRANCH + 2 * A_KV_W], axis=-1)
    return (q.reshape(b, t, A_KV_HEADS, A_GROUP, HEAD_DIM),
            k.reshape(b, t, A_KV_HEADS, HEAD_DIM),
            v.reshape(b, t, A_KV_HEADS, HEAD_DIM), gate)


def split_b(h, pos):
    b, t, _ = h.shape
    q, gate = jnp.split(h, [BRANCH], axis=-1)
    return rope(q.reshape(b, t, B_KV_HEADS, B_GROUP, HEAD_DIM), pos), gate


def shared_kv(x, w, pos):
    b, t, _ = x.shape
    k, v = jnp.split(x @ w, [B_KV_W], axis=-1)
    return (rope(k.reshape(b, t, B_KV_HEADS, HEAD_DIM), pos),
            v.reshape(b, t, B_KV_HEADS, HEAD_DIM))


def setup_inputs(seed: int = 0) -> dict:
    key = jax.random.key(seed)
    ks = jax.random.split(key, 20)
    n_pages = PAST_LEN // PAGE_SIZE
    n_used = DEC_BATCH * n_pages
    n_phys = (5 * n_used) // 4
    nrm = jax.random.normal
    s = D_MODEL ** -0.5
    f32 = jnp.float32
    x_prompt = nrm(ks[0], (BATCH, SEQ, D_MODEL), f32)
    x_sample = nrm(ks[1], (DEC_BATCH, DEC_SEQ, D_MODEL), f32)
    cache_a_k = nrm(ks[2], (N_A_LAYERS, n_phys, PAGE_SIZE, A_KV_HEADS, HEAD_DIM), f32)
    cache_a_v = nrm(ks[3], (N_A_LAYERS, n_phys, PAGE_SIZE, A_KV_HEADS, HEAD_DIM), f32)
    w_buf = min(WINDOW, PAST_LEN)
    state_b_k = nrm(ks[4], (DEC_BATCH, w_buf, B_KV_HEADS, HEAD_DIM), f32)
    state_b_v = nrm(ks[5], (DEC_BATCH, w_buf, B_KV_HEADS, HEAD_DIM), f32)
    page_table = jax.random.permutation(ks[6], n_phys)[:n_used].reshape(
        DEC_BATCH, n_pages).astype(jnp.int32)
    w_in_a = jnp.concatenate([
        nrm(ks[7], (N_A_LAYERS, D_MODEL, BRANCH), f32) * s,
        nrm(ks[8], (N_A_LAYERS, D_MODEL, A_KV_W), f32) * s,
        nrm(ks[9], (N_A_LAYERS, D_MODEL, A_KV_W), f32) * (s * DEEPNORM_BETA),
        nrm(ks[10], (N_A_LAYERS, D_MODEL, BRANCH), f32) * s,
    ], axis=-1)
    sb_bias = SB_BIAS_INIT + 0.1 * nrm(ks[19], (N_A_LAYERS, A_HEADS), f32)
    w_in_b = jnp.concatenate([
        nrm(ks[11], (N_B_LAYERS, D_MODEL, BRANCH), f32) * s,
        nrm(ks[12], (N_B_LAYERS, D_MODEL, BRANCH), f32) * s,
    ], axis=-1)
    w_kv_shared = jnp.concatenate([
        nrm(ks[13], (D_MODEL, B_KV_W), f32) * s,
        nrm(ks[14], (D_MODEL, B_KV_W), f32) * (s * DEEPNORM_BETA),
    ], axis=-1)
    w_out = nrm(ks[15], (DEPTH, BRANCH, D_MODEL), f32) * (BRANCH ** -0.5 * DEEPNORM_BETA)
    sinks = 0.5 * nrm(ks[16], (N_B_LAYERS, B_HEADS), f32)
    ln_g = 1.0 + 0.02 * nrm(ks[17], (DEPTH, D_MODEL), f32)
    ln_b = 0.02 * nrm(ks[18], (DEPTH, D_MODEL), f32)
    return {"x_prompt": x_prompt, "x_sample": x_sample,
            "cache_a_k": cache_a_k, "cache_a_v": cache_a_v,
            "state_b_k": state_b_k, "state_b_v": state_b_v,
            "page_table": page_table, "w_in_a": w_in_a, "sb_bias": sb_bias,
            "w_in_b": w_in_b, "w_kv_shared": w_kv_shared, "w_out": w_out,
            "sinks": sinks, "ln_g": ln_g, "ln_b": ln_b}


def reference(x_prompt, x_sample, cache_a_k, cache_a_v, state_b_k, state_b_v,
              page_table, w_in_a, sb_bias, w_in_b, w_kv_shared, w_out, sinks, ln_g, ln_b):
    seq_p = x_prompt.shape[1]
    dec_t = x_sample.shape[1]
    past = page_table.shape[1] * cache_a_k.shape[2]
    buf = state_b_k.shape[1]
    pos_p = jnp.arange(seq_p)
    pos_s = past + jnp.arange(dec_t)

    def post_norm(x, mixed, l):
        return layer_norm(DEEPNORM_ALPHA * x + mixed @ w_out[l], ln_g[l], ln_b[l])

    xp, xs = x_prompt, x_sample
    ak_p, av_p, ak_s, av_s = [], [], [], []
    for l in range(N_A_LAYERS):
        bias = sb_bias[l].reshape(A_KV_HEADS, A_GROUP).astype(jnp.float32)
        qp, kp, vp, gp = split_a(xp @ w_in_a[l])
        qs, ks_, vs_, gs = split_a(xs @ w_in_a[l])
        op = stick_breaking_prompt(qp, kp, vp, bias)
        os_ = stick_breaking_sample(qs, ks_, vs_,
                                    gather_pages(cache_a_k[l], page_table),
                                    gather_pages(cache_a_v[l], page_table), bias)
        xp = post_norm(xp, op * jax.nn.silu(gp), l)
        xs = post_norm(xs, os_ * jax.nn.silu(gs), l)
        ak_p.append(kp)
        av_p.append(vp)
        ak_s.append(ks_)
        av_s.append(vs_)

    kp_sh, vp_sh = shared_kv(xp, w_kv_shared, pos_p)
    ks_sh, vs_sh = shared_kv(xs, w_kv_shared, pos_s)
    k_all = jnp.concatenate([state_b_k, ks_sh], axis=1)
    v_all = jnp.concatenate([state_b_v, vs_sh], axis=1)
    k_pos_s = past - buf + jnp.arange(buf + dec_t)

    for j in range(N_B_LAYERS):
        l = N_A_LAYERS + j
        sink = sinks[j].reshape(B_KV_HEADS, B_GROUP).astype(jnp.float32)
        qp, gp = split_b(xp @ w_in_b[j], pos_p)
        qs, gs = split_b(xs @ w_in_b[j], pos_s)
        op = swa_prompt(qp, kp_sh, vp_sh, sink)
        os_ = swa_sample(qs, k_all, v_all, sink, pos_s, k_pos_s)
        xp = post_norm(xp, op * jax.nn.silu(gp), l)
        xs = post_norm(xs, os_ * jax.nn.silu(gs), l)

    w_p = min(WINDOW, seq_p)
    return (xp, xs, jnp.stack(ak_p), jnp.stack(av_p), jnp.stack(ak_s), jnp.stack(av_s),
            kp_sh[:, seq_p - w_p:], vp_sh[:, seq_p - w_p:],
            k_all[:, dec_t:], v_all[:, dec_t:])
```

```python
import functools

import numpy as np
import jax
import jax.numpy as jnp
from jax import lax
from jax.experimental import pallas as pl
from jax.experimental.pallas import tpu as pltpu

HEAD_DIM = 64
N_HEADS = 16
N_KV_HEADS = 4
GROUP = N_HEADS // N_KV_HEADS
KV_WIDTH = N_KV_HEADS * HEAD_DIM
WINDOW = 128
ROPE_THETA = 10000.0
LN_EPS = 1e-5
MASK_VALUE = -1e30
QK_SCALE = HEAD_DIM ** -0.5

SUBLANES = 8
LANES = 128
VMEM_LIMIT_BYTES = 48 * 1024 * 1024

BF16 = jnp.bfloat16
F32 = jnp.float32
NT_DIMS = (((1,), (1,)), ((), ()))


def _params(*semantics):
    return pltpu.CompilerParams(dimension_semantics=semantics,
                                vmem_limit_bytes=VMEM_LIMIT_BYTES)


def _row_tile(m):
    for t in (512, 256, 128):
        if m % t == 0:
            return t
    return m


def _softplus(z):
    return jnp.maximum(z, 0.0) + jnp.log(1.0 + jnp.exp(-jnp.abs(z)))


def _proj_a_kernel(x_ref, w_ref, q_ref, k_ref, v_ref, g_ref):
    xb = x_ref[...].astype(BF16)
    d = q_ref.shape[1]
    kw = k_ref.shape[1]
    dot = functools.partial(jnp.dot, preferred_element_type=F32)
    q_ref[...] = (dot(xb, w_ref[:, 0:d]) * QK_SCALE).astype(BF16)
    k_ref[...] = dot(xb, w_ref[:, d:d + kw])
    v_ref[...] = dot(xb, w_ref[:, d + kw:d + 2 * kw])
    g_ref[...] = dot(xb, w_ref[:, d + 2 * kw:])


def _proj_a(x, w):
    m, d = x.shape
    n = w.shape[1]
    tm = _row_tile(m)
    row = lambda i: (i, 0)
    return pl.pallas_call(
        _proj_a_kernel,
        grid=(m // tm,),
        in_specs=[pl.BlockSpec((tm, d), row), pl.BlockSpec((d, n), lambda i: (0, 0))],
        out_specs=[pl.BlockSpec((tm, d), row), pl.BlockSpec((tm, KV_WIDTH), row),
                   pl.BlockSpec((tm, KV_WIDTH), row), pl.BlockSpec((tm, d), row)],
        out_shape=[jax.ShapeDtypeStruct((m, d), BF16), jax.ShapeDtypeStruct((m, KV_WIDTH), F32),
                   jax.ShapeDtypeStruct((m, KV_WIDTH), F32), jax.ShapeDtypeStruct((m, d), F32)],
        compiler_params=_params("parallel"),
        name="proj_a",
    )(x, w)


def _proj_b_kernel(x_ref, w_ref, cos_ref, sin_ref, q_ref, g_ref):
    xb = x_ref[...].astype(BF16)
    d = q_ref.shape[1]
    dot = functools.partial(jnp.dot, preferred_element_type=F32)
    reps = d // cos_ref.shape[1]
    cos = jnp.tile(cos_ref[...], (1, reps))
    sin = jnp.tile(sin_ref[...], (1, reps))
    q = dot(xb, w_ref[:, 0:d]) * cos + dot(xb, w_ref[:, d:2 * d]) * sin
    q_ref[...] = (q * QK_SCALE).astype(BF16)
    g_ref[...] = dot(xb, w_ref[:, 2 * d:])


def _proj_b(x, w, cos, sin):
    m, d = x.shape
    n = w.shape[1]
    tm = _row_tile(m)
    row = lambda i: (i, 0)
    return pl.pallas_call(
        _proj_b_kernel,
        grid=(m // tm,),
        in_specs=[pl.BlockSpec((tm, d), row), pl.BlockSpec((d, n), lambda i: (0, 0)),
                  pl.BlockSpec((tm, LANES), row), pl.BlockSpec((tm, LANES), row)],
        out_specs=[pl.BlockSpec((tm, d), row), pl.BlockSpec((tm, d), row)],
        out_shape=[jax.ShapeDtypeStruct((m, d), BF16), jax.ShapeDtypeStruct((m, d), F32)],
        compiler_params=_params("parallel"),
        name="proj_b",
    )(x, w, cos, sin)


def _proj_kv_kernel(x_ref, w_ref, cos_ref, sin_ref, k_ref, v_ref):
    xb = x_ref[...].astype(BF16)
    kw = k_ref.shape[1]
    dot = functools.partial(jnp.dot, preferred_element_type=F32)
    reps = kw // cos_ref.shape[1]
    cos = jnp.tile(cos_ref[...], (1, reps))
    sin = jnp.tile(sin_ref[...], (1, reps))
    k_ref[...] = dot(xb, w_ref[:, 0:kw]) * cos + dot(xb, w_ref[:, kw:2 * kw]) * sin
    v_ref[...] = dot(xb, w_ref[:, 2 * kw:])


def _proj_kv(x, w, cos, sin):
    m, d = x.shape
    n = w.shape[1]
    tm = _row_tile(m)
    row = lambda i: (i, 0)
    return pl.pallas_call(
        _proj_kv_kernel,
        grid=(m // tm,),
        in_specs=[pl.BlockSpec((tm, d), row), pl.BlockSpec((d, n), lambda i: (0, 0)),
                  pl.BlockSpec((tm, LANES), row), pl.BlockSpec((tm, LANES), row)],
        out_specs=[pl.BlockSpec((tm, KV_WIDTH), row), pl.BlockSpec((tm, KV_WIDTH), row)],
        out_shape=[jax.ShapeDtypeStruct((m, KV_WIDTH), F32), jax.ShapeDtypeStruct((m, KV_WIDTH), F32)],
        compiler_params=_params("parallel"),
        name="proj_kv",
    )(x, w, cos, sin)


def _post_kernel(o_ref, g_ref, x_ref, w_ref, lng_ref, lnb_ref, out_ref, *, alpha):
    gate = g_ref[...]
    mixed = (o_ref[...] * (gate * jax.nn.sigmoid(gate))).astype(BF16)
    h = alpha * x_ref[...] + jnp.dot(mixed, w_ref[...], preferred_element_type=F32)
    mu = jnp.mean(h, axis=-1, keepdims=True)
    c = h - mu
    var = jnp.mean(c * c, axis=-1, keepdims=True)
    out_ref[...] = c * lax.rsqrt(var + LN_EPS) * lng_ref[...] + lnb_ref[...]


def _post(o, gate, x, w, ln_g, ln_b, alpha):
    m, d = x.shape
    tm = _row_tile(m)
    row = lambda i: (i, 0)
    const = lambda i: (0, 0)
    return pl.pallas_call(
        functools.partial(_post_kernel, alpha=alpha),
        grid=(m // tm,),
        in_specs=[pl.BlockSpec((tm, d), row), pl.BlockSpec((tm, d), row), pl.BlockSpec((tm, d), row),
                  pl.BlockSpec((d, d), const), pl.BlockSpec((1, d), const), pl.BlockSpec((1, d), const)],
        out_specs=pl.BlockSpec((tm, d), row),
        out_shape=jax.ShapeDtypeStruct((m, d), F32),
        compiler_params=_params("parallel"),
        name="post",
    )(o, gate, x, w, ln_g.reshape(1, d), ln_b.reshape(1, d))


SB_BLOCK = 256


def _sb_prompt_kernel(bias_ref, q_ref, k_ref, vt_ref, mask_ref, o_ref, acc_ref, carry_ref, *, blk):
    g = pl.program_id(1)
    i = pl.program_id(2)
    n_groups = blk // SUBLANES
    acc_ref[...] = jnp.zeros_like(acc_ref)
    carry_ref[...] = jnp.zeros_like(carry_ref)
    sub = lax.broadcasted_iota(jnp.int32, (SUBLANES, blk), 0)

    def kv_block(jj, masked):
        start = pl.multiple_of(jj * blk, blk)
        kb = k_ref[0, 0, pl.ds(start, blk), :]
        vb = vt_ref[0, 0, :, pl.ds(start, blk)]

        def head(r, c):
            z = lax.dot_general(kb, q_ref[0, r], NT_DIMS, preferred_element_type=F32)
            z = z + bias_ref[g * GROUP + r]
            if masked:
                z = z + mask_ref[...]
            sp = _softplus(z)
            logb = z - sp
            run = jnp.zeros((SUBLANES, blk), F32)
            pieces = [None] * n_groups
            for t in reversed(range(n_groups)):
                rows = slice(t * SUBLANES, (t + 1) * SUBLANES)
                pieces[t] = logb[rows] - run
                run = run + sp[rows]
            suffix = run
            for sh in (1, 2, 4):
                rolled = pltpu.roll(suffix, SUBLANES - sh, 0)
                suffix = suffix + jnp.where(sub < SUBLANES - sh, rolled, 0.0)
            carry = carry_ref[r]
            off = carry + (suffix - run)
            carry_ref[r] = carry + jnp.broadcast_to(suffix[0:1], (SUBLANES, blk))
            a = jnp.concatenate([jnp.exp(pc - off) for pc in pieces], axis=0).astype(BF16)
            rs = pl.multiple_of(r * HEAD_DIM, HEAD_DIM)
            acc_ref[pl.ds(rs, HEAD_DIM), :] += jnp.dot(vb, a, preferred_element_type=F32)
            return c

        lax.fori_loop(0, GROUP, head, 0)

    kv_block(i, True)

    def earlier(t, c):
        kv_block(i - 1 - t, False)
        return c

    lax.fori_loop(0, i, earlier, 0)
    o_ref[0] = acc_ref[...].T


def _sb_block_size(s):
    return SB_BLOCK if s % SB_BLOCK == 0 else LANES


def _sb_diag_mask(blk):
    rows = np.arange(blk)
    key = (rows % SUBLANES) * (blk // SUBLANES) + rows // SUBLANES
    return np.where(key[:, None] < np.arange(blk)[None, :], 0.0, MASK_VALUE).astype(np.float32)


def _sb_prompt(q, k, v, bias, b, s):
    blk = _sb_block_size(s)
    nb = s // blk
    ng = blk // SUBLANES
    qh = q.reshape(b, s, N_HEADS, HEAD_DIM).transpose(0, 2, 1, 3)
    k6 = k.astype(BF16).reshape(b, nb, SUBLANES, ng, N_KV_HEADS, HEAD_DIM)
    kperm = k6.transpose(0, 4, 1, 3, 2, 5).reshape(b, N_KV_HEADS, s, HEAD_DIM)
    v6 = v.astype(BF16).reshape(b, nb, SUBLANES, ng, N_KV_HEADS, HEAD_DIM)
    vt = v6.transpose(0, 4, 5, 1, 3, 2).reshape(b, N_KV_HEADS, HEAD_DIM, s)
    mask = jnp.asarray(_sb_diag_mask(blk))
    return pl.pallas_call(
        functools.partial(_sb_prompt_kernel, blk=blk),
        grid=(b, N_KV_HEADS, nb),
        in_specs=[
            pl.BlockSpec(memory_space=pltpu.SMEM),
            pl.BlockSpec((1, GROUP, blk, HEAD_DIM), lambda bi, g, i: (bi, g, i, 0)),
            pl.BlockSpec((1, 1, s, HEAD_DIM), lambda bi, g, i: (bi, g, 0, 0)),
            pl.BlockSpec((1, 1, HEAD_DIM, s), lambda bi, g, i: (bi, g, 0, 0)),
            pl.BlockSpec((blk, blk), lambda bi, g, i: (0, 0)),
        ],
        out_specs=pl.BlockSpec((1, blk, GROUP * HEAD_DIM), lambda bi, g, i: (bi, i, g)),
        out_shape=jax.ShapeDtypeStruct((b, s, N_HEADS * HEAD_DIM), F32),
        scratch_shapes=[pltpu.VMEM((GROUP * HEAD_DIM, blk), F32),
                        pltpu.VMEM((GROUP, SUBLANES, blk), F32)],
        compiler_params=_params("parallel", "parallel", "arbitrary"),
        name="sb_prompt",
    )(bias, qh, kperm, vt, mask)


def _block_diag_queries(q):
    n = q.shape[0]
    q4 = q.reshape(n, N_HEADS, 1, HEAD_DIM)
    sel = (np.arange(N_HEADS)[:, None] // GROUP == np.arange(N_KV_HEADS)[None, :])
    sel = jnp.asarray(sel.reshape(1, N_HEADS, N_KV_HEADS, 1), q.dtype)
    return (q4 * sel).reshape(n, N_HEADS, KV_WIDTH)


def _own_kv_head_columns(o):
    n = o.shape[0]
    o5 = o.reshape(n, N_KV_HEADS, GROUP, N_KV_HEADS, HEAD_DIM)
    own = jnp.stack([o5[:, g, :, g, :] for g in range(N_KV_HEADS)], axis=1)
    return own.reshape(n, N_HEADS * HEAD_DIM)


def _split3(x):
    hi = x.astype(BF16)
    r1 = x - hi.astype(F32)
    mid = r1.astype(BF16)
    lo = (r1 - mid.astype(F32)).astype(BF16)
    return hi, mid, lo


def _sb_decode_kernel(pt_ref, q_ref, bias_ref, tri_ref, *refs, pages):
    k_refs = refs[:pages]
    v_refs = refs[pages:2 * pages]
    o_ref, carry_ref, acc_ref = refs[2 * pages:]
    c = pl.program_id(1)

    @pl.when(c == 0)
    def _():
        carry_ref[...] = jnp.zeros_like(carry_ref)
        acc_ref[...] = jnp.zeros_like(acc_ref)

    qbd = q_ref[0]
    tri = tri_ref[...]
    carry = carry_ref[...]
    acc = acc_ref[...]
    dot = functools.partial(jnp.dot, preferred_element_type=F32)
    for j in reversed(range(pages)):
        z = dot(qbd, k_refs[j][0, 0].astype(BF16)) + bias_ref[...]
        sp = _softplus(z)
        logb = z - sp
        hi, mid, lo = _split3(sp)
        suffix = dot(hi, tri) + dot(mid, tri) + dot(lo, tri)
        a = jnp.exp(logb - (suffix - sp) - carry)
        carry = carry + jnp.broadcast_to(suffix[:, 0:1], carry.shape)
        acc = acc + lax.dot_general(a.astype(BF16), v_refs[j][0, 0].astype(BF16), NT_DIMS,
                                    preferred_element_type=F32)
    carry_ref[...] = carry
    acc_ref[...] = acc

    @pl.when(c == pl.num_programs(1) - 1)
    def _():
        o_ref[0] = acc


def _sb_decode(q, cache_kt, cache_vt, layer, page_table, bias):
    n = q.shape[0]
    n_pages = page_table.shape[1]
    page = cache_kt.shape[-1]
    pages = 16 if n_pages % 16 == 0 else n_pages
    n_chunks = n_pages // pages
    qbd = _block_diag_queries(q)
    bias_b = jnp.broadcast_to(bias.astype(F32)[:, None], (N_HEADS, page))
    tri = jnp.asarray(np.tril(np.ones((page, page), np.float32)), BF16)

    def page_spec(j):
        def index(bi, c, pt):
            return (layer, pt[bi, (n_chunks - 1 - c) * pages + j], 0, 0)
        return pl.BlockSpec((1, 1, KV_WIDTH, page), index)

    grid_spec = pltpu.PrefetchScalarGridSpec(
        num_scalar_prefetch=1,
        grid=(n, n_chunks),
        in_specs=[pl.BlockSpec((1, N_HEADS, KV_WIDTH), lambda bi, c, pt: (bi, 0, 0)),
                  pl.BlockSpec((N_HEADS, page), lambda bi, c, pt: (0, 0)),
                  pl.BlockSpec((page, page), lambda bi, c, pt: (0, 0))]
                 + [page_spec(j) for j in range(pages)] * 2,
        out_specs=pl.BlockSpec((1, N_HEADS, KV_WIDTH), lambda bi, c, pt: (bi, 0, 0)),
        scratch_shapes=[pltpu.VMEM((N_HEADS, page), F32), pltpu.VMEM((N_HEADS, KV_WIDTH), F32)],
    )
    out = pl.pallas_call(
        functools.partial(_sb_decode_kernel, pages=pages),
        grid_spec=grid_spec,
        out_shape=jax.ShapeDtypeStruct((n, N_HEADS, KV_WIDTH), F32),
        compiler_params=_params("parallel", "arbitrary"),
        name="sb_decode",
    )(page_table, qbd, bias_b, tri, *([cache_kt] * pages), *([cache_vt] * pages))
    return _own_kv_head_columns(out)


def _swa_prompt_kernel(sink_ref, q_ref, kp_ref, kc_ref, vp_ref, vc_ref, o_ref):
    g = pl.program_id(1)
    n = pl.program_id(2)
    blk = kc_ref.shape[2]
    kb = jnp.concatenate([kp_ref[0, 0], kc_ref[0, 0]], axis=0)
    vb = jnp.concatenate([vp_ref[0, 0], vc_ref[0, 0]], axis=0)
    t = lax.broadcasted_iota(jnp.int32, (blk, 2 * blk), 0)
    sk = lax.broadcasted_iota(jnp.int32, (blk, 2 * blk), 1)
    diff = t + blk - sk
    valid = (diff >= 0) & (diff <= WINDOW) & ((sk >= blk) | (n > 0))
    for r in range(GROUP):
        sc = lax.dot_general(q_ref[0, r], kb, NT_DIMS, preferred_element_type=F32)
        sc = jnp.where(valid, sc, MASK_VALUE)
        sink = sink_ref[g * GROUP + r]
        m = jnp.maximum(jnp.max(sc, axis=-1, keepdims=True), sink)
        p = jnp.exp(sc - m)
        den = jnp.sum(p, axis=-1, keepdims=True) + jnp.exp(sink - m)
        o_ref[0, r] = jnp.dot((p / den).astype(BF16), vb, preferred_element_type=F32)


def _to_heads(x, b, s, heads):
    return x.reshape(b, s, heads, HEAD_DIM).transpose(0, 2, 1, 3)


def _swa_prompt(q, kh, vh, sink, b, s):
    blk = WINDOW
    nb = s // blk
    qh = _to_heads(q, b, s, N_HEADS)
    prev = lambda bi, g, n: (bi, g, jnp.maximum(n - 1, 0), 0)
    cur = lambda bi, g, n: (bi, g, n, 0)
    kv_block = (1, 1, blk, HEAD_DIM)
    out = pl.pallas_call(
        _swa_prompt_kernel,
        grid=(b, N_KV_HEADS, nb),
        in_specs=[pl.BlockSpec(memory_space=pltpu.SMEM),
                  pl.BlockSpec((1, GROUP, blk, HEAD_DIM), cur),
                  pl.BlockSpec(kv_block, prev), pl.BlockSpec(kv_block, cur),
                  pl.BlockSpec(kv_block, prev), pl.BlockSpec(kv_block, cur)],
        out_specs=pl.BlockSpec((1, GROUP, blk, HEAD_DIM), cur),
        out_shape=jax.ShapeDtypeStruct((b, N_HEADS, s, HEAD_DIM), F32),
        compiler_params=_params("parallel", "parallel", "parallel"),
        name="swa_prompt",
    )(sink, qh, kh, kh, vh, vh)
    return out.transpose(0, 2, 1, 3).reshape(b * s, N_HEADS * HEAD_DIM)


def _swa_decode_kernel(qbd_ref, q_ref, knew_ref, vnew_ref, sink_ref, kt_ref, vt_ref, o_ref):
    z = jnp.dot(qbd_ref[0], kt_ref[0].astype(BF16), preferred_element_type=F32)
    zn = jnp.sum(q_ref[0].astype(F32) * knew_ref[0], axis=-1, keepdims=True)
    sink = sink_ref[...]
    m = jnp.maximum(jnp.maximum(jnp.max(z, axis=-1, keepdims=True), zn), sink)
    p = jnp.exp(z - m)
    pn = jnp.exp(zn - m)
    inv = 1.0 / (jnp.sum(p, axis=-1, keepdims=True) + pn + jnp.exp(sink - m))
    out = lax.dot_general((p * inv).astype(BF16), vt_ref[0].astype(BF16), NT_DIMS,
                          preferred_element_type=F32)
    o_ref[0] = out + (pn * inv) * vnew_ref[0]


def _swa_decode(q, k_new, v_new, state_kt, state_vt, sink):
    n = q.shape[0]
    buf = state_kt.shape[-1]
    qbd = _block_diag_queries(q)
    q16 = q.reshape(n, N_HEADS, HEAD_DIM)
    knew16 = jnp.repeat(k_new.reshape(n, N_KV_HEADS, HEAD_DIM), GROUP, axis=1)
    per_n = lambda i: (i, 0, 0)
    out = pl.pallas_call(
        _swa_decode_kernel,
        grid=(n,),
        in_specs=[pl.BlockSpec((1, N_HEADS, KV_WIDTH), per_n),
                  pl.BlockSpec((1, N_HEADS, HEAD_DIM), per_n),
                  pl.BlockSpec((1, N_HEADS, HEAD_DIM), per_n),
                  pl.BlockSpec((1, 1, KV_WIDTH), per_n),
                  pl.BlockSpec((N_HEADS, 1), lambda i: (0, 0)),
                  pl.BlockSpec((1, KV_WIDTH, buf), per_n),
                  pl.BlockSpec((1, KV_WIDTH, buf), per_n)],
        out_specs=pl.BlockSpec((1, N_HEADS, KV_WIDTH), per_n),
        out_shape=jax.ShapeDtypeStruct((n, N_HEADS, KV_WIDTH), F32),
        compiler_params=_params("parallel"),
        name="swa_decode",
    )(qbd, q16, knew16, v_new.reshape(n, 1, KV_WIDTH), sink.astype(F32).reshape(N_HEADS, 1),
      state_kt, state_vt)
    return _own_kv_head_columns(out)


def _rope_tables(pos):
    half = HEAD_DIM // 2
    inv = ROPE_THETA ** (-jnp.arange(half, dtype=F32) * 2.0 / HEAD_DIM)
    ang = pos.astype(F32)[:, None] * inv[None, :]
    reps = LANES // half
    return jnp.tile(jnp.cos(ang), (1, reps)), jnp.tile(jnp.sin(ang), (1, reps))


def _with_rotate_half(w, rest):
    d = w.shape[0]
    w4 = w.reshape(d, -1, 2, HEAD_DIM // 2)
    rot = jnp.concatenate([-w4[:, :, 1:2], w4[:, :, 0:1]], axis=2).reshape(d, -1)
    return jnp.concatenate([w, rot, rest], axis=1).astype(BF16)


def kernel(x_prompt, x_sample, cache_a_k, cache_a_v, state_b_k, state_b_v, page_table, w_in_a,
           sb_bias, w_in_b, w_kv_shared, w_out, sinks, ln_g, ln_b):
    b, s, d = x_prompt.shape
    n, dec_t, _ = x_sample.shape
    assert dec_t == 1 and d == N_HEADS * HEAD_DIM
    n_a = w_in_a.shape[0]
    n_b = w_in_b.shape[0]
    depth = w_out.shape[0]
    alpha = (2.0 * depth) ** 0.25
    page = cache_a_k.shape[2]
    past = page_table.shape[1] * page
    buf = state_b_k.shape[1]
    assert buf == WINDOW and s % WINDOW == 0

    xp = x_prompt.reshape(b * s, d)
    xs = x_sample.reshape(n, d)
    w_out_bf = w_out.astype(BF16)
    cache_kt = cache_a_k.transpose(0, 1, 3, 4, 2).reshape(n_a, -1, KV_WIDTH, page)
    cache_vt = cache_a_v.transpose(0, 1, 3, 4, 2).reshape(n_a, -1, KV_WIDTH, page)

    ak_p, av_p, ak_s, av_s = [], [], [], []
    for l in range(n_a):
        wa = w_in_a[l].astype(BF16)
        qp, kp, vp, gp = _proj_a(xp, wa)
        qs, ks, vs, gs = _proj_a(xs, wa)
        op = _sb_prompt(qp, kp, vp, sb_bias[l].astype(F32), b, s).reshape(b * s, d)
        os_ = _sb_decode(qs, cache_kt, cache_vt, l, page_table, sb_bias[l])
        xp = _post(op, gp, xp, w_out_bf[l], ln_g[l], ln_b[l], alpha)
        xs = _post(os_, gs, xs, w_out_bf[l], ln_g[l], ln_b[l], alpha)
        ak_p.append(kp.reshape(b, s, N_KV_HEADS, HEAD_DIM))
        av_p.append(vp.reshape(b, s, N_KV_HEADS, HEAD_DIM))
        ak_s.append(ks.reshape(n, 1, N_KV_HEADS, HEAD_DIM))
        av_s.append(vs.reshape(n, 1, N_KV_HEADS, HEAD_DIM))

    cos_p, sin_p = _rope_tables(jnp.arange(s))
    cos_p, sin_p = jnp.tile(cos_p, (b, 1)), jnp.tile(sin_p, (b, 1))
    cos_s, sin_s = _rope_tables(jnp.full((n,), past))

    w_kv = _with_rotate_half(w_kv_shared[:, :KV_WIDTH], w_kv_shared[:, KV_WIDTH:])
    kp_sh, vp_sh = _proj_kv(xp, w_kv, cos_p, sin_p)
    ks_sh, vs_sh = _proj_kv(xs, w_kv, cos_s, sin_s)
    kh = _to_heads(kp_sh.astype(BF16), b, s, N_KV_HEADS)
    vh = _to_heads(vp_sh.astype(BF16), b, s, N_KV_HEADS)
    state_kt = state_b_k.transpose(0, 2, 3, 1).reshape(n, KV_WIDTH, buf)
    state_vt = state_b_v.transpose(0, 2, 3, 1).reshape(n, KV_WIDTH, buf)

    for j in range(n_b):
        l = n_a + j
        wb = _with_rotate_half(w_in_b[j][:, :d], w_in_b[j][:, d:])
        qp, gp = _proj_b(xp, wb, cos_p, sin_p)
        qs, gs = _proj_b(xs, wb, cos_s, sin_s)
        op = _swa_prompt(qp, kh, vh, sinks[j].astype(F32), b, s)
        os_ = _swa_decode(qs, ks_sh, vs_sh, state_kt, state_vt, sinks[j])
        xp = _post(op, gp, xp, w_out_bf[l], ln_g[l], ln_b[l], alpha)
        xs = _post(os_, gs, xs, w_out_bf[l], ln_g[l], ln_b[l], alpha)

    kp4 = kp_sh.reshape(b, s, N_KV_HEADS, HEAD_DIM)
    vp4 = vp_sh.reshape(b, s, N_KV_HEADS, HEAD_DIM)
    ks4 = ks_sh.reshape(n, 1, N_KV_HEADS, HEAD_DIM)
    vs4 = vs_sh.reshape(n, 1, N_KV_HEADS, HEAD_DIM)
    w_p = min(WINDOW, s)
    return (xp.reshape(b, s, d), xs.reshape(n, 1, d),
            jnp.stack(ak_p), jnp.stack(av_p), jnp.stack(ak_s), jnp.stack(av_s),
            kp4[:, s - w_p:], vp4[:, s - w_p:],
            jnp.concatenate([state_b_k, ks4], axis=1)[:, dec_t:],
            jnp.concatenate([state_b_v, vs4], axis=1)[:, dec_t:])
```

```python
import functools

import numpy as np
import jax
import jax.numpy as jnp
from jax import lax
from jax.experimental import pallas as pl
from jax.experimental.pallas import tpu as pltpu

HEAD_DIM = 64
N_HEADS = 16
N_KV_HEADS = 4
GROUP = N_HEADS // N_KV_HEADS
KV_WIDTH = N_KV_HEADS * HEAD_DIM
WINDOW = 128
ROPE_THETA = 10000.0
LN_EPS = 1e-5
MASK_VALUE = -1e30
QK_SCALE = HEAD_DIM ** -0.5

SUBLANES = 8
LANES = 128
VMEM_LIMIT_BYTES = 48 * 1024 * 1024

BF16 = jnp.bfloat16
F32 = jnp.float32
NT_DIMS = (((1,), (1,)), ((), ()))


def _params(*semantics):
    return pltpu.CompilerParams(dimension_semantics=semantics,
                                vmem_limit_bytes=VMEM_LIMIT_BYTES)


def _row_tile(m):
    for t in (512, 256, 128):
        if m % t == 0:
            return t
    return m


def _softplus(z):
    return jnp.maximum(z, 0.0) + jnp.log(1.0 + jnp.exp(-jnp.abs(z)))


def _proj_a_kernel(x_ref, w_ref, q_ref, k_ref, v_ref, g_ref):
    xb = x_ref[...].astype(BF16)
    d = q_ref.shape[1]
    kw = k_ref.shape[1]
    dot = functools.partial(jnp.dot, preferred_element_type=F32)
    q_ref[...] = (dot(xb, w_ref[:, 0:d]) * QK_SCALE).astype(BF16)
    k_ref[...] = dot(xb, w_ref[:, d:d + kw])
    v_ref[...] = dot(xb, w_ref[:, d + kw:d + 2 * kw])
    g_ref[...] = dot(xb, w_ref[:, d + 2 * kw:])


def _proj_a(x, w):
    m, d = x.shape
    n = w.shape[1]
    tm = _row_tile(m)
    row = lambda i: (i, 0)
    return pl.pallas_call(
        _proj_a_kernel,
        grid=(m // tm,),
        in_specs=[pl.BlockSpec((tm, d), row), pl.BlockSpec((d, n), lambda i: (0, 0))],
        out_specs=[pl.BlockSpec((tm, d), row), pl.BlockSpec((tm, KV_WIDTH), row),
                   pl.BlockSpec((tm, KV_WIDTH), row), pl.BlockSpec((tm, d), row)],
        out_shape=[jax.ShapeDtypeStruct((m, d), BF16), jax.ShapeDtypeStruct((m, KV_WIDTH), F32),
                   jax.ShapeDtypeStruct((m, KV_WIDTH), F32), jax.ShapeDtypeStruct((m, d), F32)],
        compiler_params=_params("parallel"),
        name="proj_a",
    )(x, w)


def _proj_b_kernel(x_ref, w_ref, cos_ref, sin_ref, q_ref, g_ref):
    xb = x_ref[...].astype(BF16)
    d = q_ref.shape[1]
    dot = functools.partial(jnp.dot, preferred_element_type=F32)
    reps = d // cos_ref.shape[1]
    cos = jnp.tile(cos_ref[...], (1, reps))
    sin = jnp.tile(sin_ref[...], (1, reps))
    q = dot(xb, w_ref[:, 0:d]) * cos + dot(xb, w_ref[:, d:2 * d]) * sin
    q_ref[...] = (q * QK_SCALE).astype(BF16)
    g_ref[...] = dot(xb, w_ref[:, 2 * d:])


def _proj_b(x, w, cos, sin):
    m, d = x.shape
    n = w.shape[1]
    tm = _row_tile(m)
    row = lambda i: (i, 0)
    return pl.pallas_call(
        _proj_b_kernel,
        grid=(m // tm,),
        in_specs=[pl.BlockSpec((tm, d), row), pl.BlockSpec((d, n), lambda i: (0, 0)),
                  pl.BlockSpec((tm, LANES), row), pl.BlockSpec((tm, LANES), row)],
        out_specs=[pl.BlockSpec((tm, d), row), pl.BlockSpec((tm, d), row)],
        out_shape=[jax.ShapeDtypeStruct((m, d), BF16), jax.ShapeDtypeStruct((m, d), F32)],
        compiler_params=_params("parallel"),
        name="proj_b",
    )(x, w, cos, sin)


def _proj_kv_kernel(x_ref, w_ref, cos_ref, sin_ref, k_ref, v_ref):
    xb = x_ref[...].astype(BF16)
    kw = k_ref.shape[1]
    dot = functools.partial(jnp.dot, preferred_element_type=F32)
    reps = kw // cos_ref.shape[1]
    cos = jnp.tile(cos_ref[...], (1, reps))
    sin = jnp.tile(sin_ref[...], (1, reps))
    k_ref[...] = dot(xb, w_ref[:, 0:kw]) * cos + dot(xb, w_ref[:, kw:2 * kw]) * sin
    v_ref[...] = dot(xb, w_ref[:, 2 * kw:])


def _proj_kv(x, w, cos, sin):
    m, d = x.shape
    n = w.shape[1]
    tm = _row_tile(m)
    row = lambda i: (i, 0)
    return pl.pallas_call(
        _proj_kv_kernel,
        grid=(m // tm,),
        in_specs=[pl.BlockSpec((tm, d), row), pl.BlockSpec((d, n), lambda i: (0, 0)),
                  pl.BlockSpec((tm, LANES), row), pl.BlockSpec((tm, LANES), row)],
        out_specs=[pl.BlockSpec((tm, KV_WIDTH), row), pl.BlockSpec((tm, KV_WIDTH), row)],
        out_shape=[jax.ShapeDtypeStruct((m, KV_WIDTH), F32), jax.ShapeDtypeStruct((m, KV_WIDTH), F32)],
        compiler_params=_params("parallel"),
        name="proj_kv",
    )(x, w, cos, sin)


def _post_kernel(o_ref, g_ref, x_ref, w_ref, lng_ref, lnb_ref, out_ref, *, alpha):
    gate = g_ref[...]
    mixed = (o_ref[...] * (gate * jax.nn.sigmoid(gate))).astype(BF16)
    h = alpha * x_ref[...] + jnp.dot(mixed, w_ref[...], preferred_element_type=F32)
    mu = jnp.mean(h, axis=-1, keepdims=True)
    c = h - mu
    var = jnp.mean(c * c, axis=-1, keepdims=True)
    out_ref[...] = c * lax.rsqrt(var + LN_EPS) * lng_ref[...] + lnb_ref[...]


def _post(o, gate, x, w, ln_g, ln_b, alpha):
    m, d = x.shape
    tm = _row_tile(m)
    row = lambda i: (i, 0)
    const = lambda i: (0, 0)
    return pl.pallas_call(
        functools.partial(_post_kernel, alpha=alpha),
        grid=(m // tm,),
        in_specs=[pl.BlockSpec((tm, d), row), pl.BlockSpec((tm, d), row), pl.BlockSpec((tm, d), row),
                  pl.BlockSpec((d, d), const), pl.BlockSpec((1, d), const), pl.BlockSpec((1, d), const)],
        out_specs=pl.BlockSpec((tm, d), row),
        out_shape=jax.ShapeDtypeStruct((m, d), F32),
        compiler_params=_params("parallel"),
        name="post",
    )(o, gate, x, w, ln_g.reshape(1, d), ln_b.reshape(1, d))


SB_BLOCK = 256


LOG2E = 1.4426950408889634
SB_AUG = 2 * HEAD_DIM


def _sb_prompt_kernel(q_ref, k_ref, vt_ref, mask_ref, o_ref, acc_ref, carry_ref, z_ref, p_ref, *,
                      blk):
    i = pl.program_id(2)
    n_groups = blk // SUBLANES
    width = GROUP * blk
    acc_ref[...] = jnp.zeros_like(acc_ref)
    carry_ref[...] = jnp.zeros_like(carry_ref)
    sub = lax.broadcasted_iota(jnp.int32, (SUBLANES, width), 0)
    q4 = q_ref[0].reshape(width, SB_AUG)

    def logits(jj):
        start = pl.multiple_of(jj * blk, blk)
        kb = k_ref[0, 0, pl.ds(start, blk), :]
        z_ref[...] = lax.dot_general(kb, q4, NT_DIMS, preferred_element_type=F32)

    def kv_block(jj, masked):
        start = pl.multiple_of(jj * blk, blk)
        vb = vt_ref[0, 0, :, pl.ds(start, blk)]
        run = jnp.zeros((SUBLANES, width), F32)
        for t in reversed(range(n_groups)):
            rows = slice(t * SUBLANES, (t + 1) * SUBLANES)
            zt = z_ref[rows, :]
            if masked:
                zt = zt + jnp.tile(mask_ref[rows, :], (1, GROUP))
            e = jnp.exp2(jnp.abs(zt) * (-LOG2E))
            run = run + (jnp.maximum(zt, 0.0) + jnp.log(1.0 + e))
            p_ref[rows, :] = zt - run
        logits(jnp.maximum(jj - 1, 0))
        suffix = run
        for sh in (1, 2, 4):
            rolled = pltpu.roll(suffix, SUBLANES - sh, 0)
            suffix = suffix + jnp.where(sub < SUBLANES - sh, rolled, 0.0)
        carry = carry_ref[...]
        off = (carry + (suffix - run)) * LOG2E
        carry_ref[...] = carry + jnp.broadcast_to(suffix[0:1], (SUBLANES, width))
        for r in range(GROUP):
            cols = slice(r * blk, (r + 1) * blk)
            off_r = off[:, cols]
            a = jnp.concatenate(
                [jnp.exp2(p_ref[t * SUBLANES:(t + 1) * SUBLANES, cols] * LOG2E - off_r)
                 for t in range(n_groups)], axis=0).astype(BF16)
            acc_ref[r] += jnp.dot(vb, a, preferred_element_type=F32)

    logits(i)
    kv_block(i, True)

    def earlier(t, c):
        kv_block(i - 1 - t, False)
        return c

    lax.fori_loop(0, i, earlier, 0)
    o_ref[0] = acc_ref[...].reshape(GROUP * HEAD_DIM, blk).T


def _sb_block_size(s):
    return SB_BLOCK if s % SB_BLOCK == 0 else LANES


def _sb_diag_mask(blk):
    rows = np.arange(blk)
    key = (rows % SUBLANES) * (blk // SUBLANES) + rows // SUBLANES
    return np.where(key[:, None] < np.arange(blk)[None, :], 0.0, MASK_VALUE).astype(np.float32)


def _sb_prompt(q, k, v, bias, b, s):
    blk = _sb_block_size(s)
    nb = s // blk
    ng = blk // SUBLANES
    pad = SB_AUG - HEAD_DIM - 3
    qh = q.reshape(b, s, N_HEADS, HEAD_DIM).transpose(0, 2, 1, 3)
    bias3 = jnp.stack(_split3(bias.astype(F32)), axis=-1)
    bias_cols = jnp.pad(bias3, ((0, 0), (0, pad)))[None, :, None, :]
    q_aug = jnp.concatenate([qh, jnp.broadcast_to(bias_cols, (b, N_HEADS, s, SB_AUG - HEAD_DIM))],
                            axis=-1)
    k6 = k.astype(BF16).reshape(b, nb, SUBLANES, ng, N_KV_HEADS, HEAD_DIM)
    kperm = k6.transpose(0, 4, 1, 3, 2, 5).reshape(b, N_KV_HEADS, s, HEAD_DIM)
    ones_cols = jnp.pad(jnp.ones((3,), BF16), (0, pad))
    k_aug = jnp.concatenate(
        [kperm, jnp.broadcast_to(ones_cols, (b, N_KV_HEADS, s, SB_AUG - HEAD_DIM))], axis=-1)
    v6 = v.astype(BF16).reshape(b, nb, SUBLANES, ng, N_KV_HEADS, HEAD_DIM)
    vt = v6.transpose(0, 4, 5, 1, 3, 2).reshape(b, N_KV_HEADS, HEAD_DIM, s)
    mask = jnp.asarray(_sb_diag_mask(blk))
    return pl.pallas_call(
        functools.partial(_sb_prompt_kernel, blk=blk),
        grid=(b, N_KV_HEADS, nb),
        in_specs=[
            pl.BlockSpec((1, GROUP, blk, SB_AUG), lambda bi, g, i: (bi, g, i, 0)),
            pl.BlockSpec((1, 1, s, SB_AUG), lambda bi, g, i: (bi, g, 0, 0)),
            pl.BlockSpec((1, 1, HEAD_DIM, s), lambda bi, g, i: (bi, g, 0, 0)),
            pl.BlockSpec((blk, blk), lambda bi, g, i: (0, 0)),
        ],
        out_specs=pl.BlockSpec((1, blk, GROUP * HEAD_DIM), lambda bi, g, i: (bi, i, g)),
        out_shape=jax.ShapeDtypeStruct((b, s, N_HEADS * HEAD_DIM), F32),
        scratch_shapes=[pltpu.VMEM((GROUP, HEAD_DIM, blk), F32),
                        pltpu.VMEM((SUBLANES, GROUP * blk), F32),
                        pltpu.VMEM((blk, GROUP * blk), F32),
                        pltpu.VMEM((blk, GROUP * blk), F32)],
        compiler_params=_params("parallel", "parallel", "arbitrary"),
        name="sb_prompt",
    )(q_aug, k_aug, vt, mask)


SB_DECODE_PAGES = 32


def _block_diag_queries(q):
    n = q.shape[0]
    q4 = q.reshape(n, N_HEADS, 1, HEAD_DIM)
    sel = (np.arange(N_HEADS)[:, None] // GROUP == np.arange(N_KV_HEADS)[None, :])
    sel = jnp.asarray(sel.reshape(1, N_HEADS, N_KV_HEADS, 1), q.dtype)
    return (q4 * sel).reshape(n, N_HEADS, KV_WIDTH)


def _own_kv_head_columns(o):
    n = o.shape[0]
    o5 = o.reshape(n, N_KV_HEADS, GROUP, N_KV_HEADS, HEAD_DIM)
    own = jnp.stack([o5[:, g, :, g, :] for g in range(N_KV_HEADS)], axis=1)
    return own.reshape(n, N_HEADS * HEAD_DIM)


def _split3(x):
    hi = x.astype(BF16)
    r1 = x - hi.astype(F32)
    mid = r1.astype(BF16)
    lo = (r1 - mid.astype(F32)).astype(BF16)
    return hi, mid, lo


def _sb_decode_kernel(pt_ref, q_ref, bias_ref, tri_ref, *refs, pages):
    k_refs = refs[:pages]
    v_refs = refs[pages:2 * pages]
    o_ref, carry_ref, acc_ref = refs[2 * pages:]
    c = pl.program_id(1)

    @pl.when(c == 0)
    def _():
        carry_ref[...] = jnp.zeros_like(carry_ref)
        acc_ref[...] = jnp.zeros_like(acc_ref)

    qbd = q_ref[0]
    tri = tri_ref[...]
    dot = functools.partial(jnp.dot, preferred_element_type=F32)
    z = jnp.concatenate([dot(qbd, k_refs[j][0, 0].astype(BF16)) for j in range(pages)], axis=0)
    z = z + jnp.tile(bias_ref[...], (pages, 1))
    sp = _softplus(z)
    hi, mid, lo = _split3(sp)
    suffix = dot(hi, tri) + dot(mid, tri) + dot(lo, tri)
    logw = z - suffix
    totals = jnp.broadcast_to(suffix[:, 0:1], suffix.shape)
    carry = carry_ref[...]
    acc = acc_ref[...]
    for j in reversed(range(pages)):
        rows = slice(j * N_HEADS, (j + 1) * N_HEADS)
        a = jnp.exp(logw[rows] - carry)
        carry = carry + totals[rows]
        acc = acc + lax.dot_general(a.astype(BF16), v_refs[j][0, 0].astype(BF16), NT_DIMS,
                                    preferred_element_type=F32)
    carry_ref[...] = carry
    acc_ref[...] = acc

    @pl.when(c == pl.num_programs(1) - 1)
    def _():
        o_ref[0] = acc


def _sb_decode(q, cache_kt, cache_vt, layer, page_table, bias):
    n = q.shape[0]
    n_pages = page_table.shape[1]
    page = cache_kt.shape[-1]
    pages = SB_DECODE_PAGES if n_pages % SB_DECODE_PAGES == 0 else n_pages
    n_chunks = n_pages // pages
    qbd = _block_diag_queries(q)
    bias_b = jnp.broadcast_to(bias.astype(F32)[:, None], (N_HEADS, page))
    tri = jnp.asarray(np.tril(np.ones((page, page), np.float32)), BF16)

    def page_spec(j):
        def index(bi, c, pt):
            return (layer, pt[bi, (n_chunks - 1 - c) * pages + j], 0, 0)
        return pl.BlockSpec((1, 1, KV_WIDTH, page), index)

    grid_spec = pltpu.PrefetchScalarGridSpec(
        num_scalar_prefetch=1,
        grid=(n, n_chunks),
        in_specs=[pl.BlockSpec((1, N_HEADS, KV_WIDTH), lambda bi, c, pt: (bi, 0, 0)),
                  pl.BlockSpec((N_HEADS, page), lambda bi, c, pt: (0, 0)),
                  pl.BlockSpec((page, page), lambda bi, c, pt: (0, 0))]
                 + [page_spec(j) for j in range(pages)] * 2,
        out_specs=pl.BlockSpec((1, N_HEADS, KV_WIDTH), lambda bi, c, pt: (bi, 0, 0)),
        scratch_shapes=[pltpu.VMEM((N_HEADS, page), F32), pltpu.VMEM((N_HEADS, KV_WIDTH), F32)],
    )
    out = pl.pallas_call(
        functools.partial(_sb_decode_kernel, pages=pages),
        grid_spec=grid_spec,
        out_shape=jax.ShapeDtypeStruct((n, N_HEADS, KV_WIDTH), F32),
        compiler_params=_params("parallel", "arbitrary"),
        name="sb_decode",
    )(page_table, qbd, bias_b, tri, *([cache_kt] * pages), *([cache_vt] * pages))
    return _own_kv_head_columns(out)


def _swa_prompt_kernel(sink_ref, q_ref, kp_ref, kc_ref, vp_ref, vc_ref, o_ref):
    g = pl.program_id(1)
    n = pl.program_id(2)
    blk = kc_ref.shape[2]
    kb = jnp.concatenate([kp_ref[0, 0], kc_ref[0, 0]], axis=0)
    vb = jnp.concatenate([vp_ref[0, 0], vc_ref[0, 0]], axis=0)
    t = lax.broadcasted_iota(jnp.int32, (blk, 2 * blk), 0)
    sk = lax.broadcasted_iota(jnp.int32, (blk, 2 * blk), 1)
    diff = t + blk - sk
    valid = (diff >= 0) & (diff <= WINDOW) & ((sk >= blk) | (n > 0))
    for r in range(GROUP):
        sc = lax.dot_general(q_ref[0, r], kb, NT_DIMS, preferred_element_type=F32)
        sc = jnp.where(valid, sc, MASK_VALUE)
        sink = sink_ref[g * GROUP + r]
        m = jnp.maximum(jnp.max(sc, axis=-1, keepdims=True), sink)
        p = jnp.exp(sc - m)
        den = jnp.sum(p, axis=-1, keepdims=True) + jnp.exp(sink - m)
        o_ref[0, r] = jnp.dot((p / den).astype(BF16), vb, preferred_element_type=F32)


def _to_heads(x, b, s, heads):
    return x.reshape(b, s, heads, HEAD_DIM).transpose(0, 2, 1, 3)


def _swa_prompt(q, kh, vh, sink, b, s):
    blk = WINDOW
    nb = s // blk
    qh = _to_heads(q, b, s, N_HEADS)
    prev = lambda bi, g, n: (bi, g, jnp.maximum(n - 1, 0), 0)
    cur = lambda bi, g, n: (bi, g, n, 0)
    kv_block = (1, 1, blk, HEAD_DIM)
    out = pl.pallas_call(
        _swa_prompt_kernel,
        grid=(b, N_KV_HEADS, nb),
        in_specs=[pl.BlockSpec(memory_space=pltpu.SMEM),
                  pl.BlockSpec((1, GROUP, blk, HEAD_DIM), cur),
                  pl.BlockSpec(kv_block, prev), pl.BlockSpec(kv_block, cur),
                  pl.BlockSpec(kv_block, prev), pl.BlockSpec(kv_block, cur)],
        out_specs=pl.BlockSpec((1, GROUP, blk, HEAD_DIM), cur),
        out_shape=jax.ShapeDtypeStruct((b, N_HEADS, s, HEAD_DIM), F32),
        compiler_params=_params("parallel", "parallel", "parallel"),
        name="swa_prompt",
    )(sink, qh, kh, kh, vh, vh)
    return out.transpose(0, 2, 1, 3).reshape(b * s, N_HEADS * HEAD_DIM)


def _swa_decode_kernel(qbd_ref, q_ref, knew_ref, vnew_ref, sink_ref, kt_ref, vt_ref, o_ref):
    z = jnp.dot(qbd_ref[0], kt_ref[0].astype(BF16), preferred_element_type=F32)
    zn = jnp.sum(q_ref[0].astype(F32) * knew_ref[0], axis=-1, keepdims=True)
    sink = sink_ref[...]
    m = jnp.maximum(jnp.maximum(jnp.max(z, axis=-1, keepdims=True), zn), sink)
    p = jnp.exp(z - m)
    pn = jnp.exp(zn - m)
    inv = 1.0 / (jnp.sum(p, axis=-1, keepdims=True) + pn + jnp.exp(sink - m))
    out = lax.dot_general((p * inv).astype(BF16), vt_ref[0].astype(BF16), NT_DIMS,
                          preferred_element_type=F32)
    o_ref[0] = out + (pn * inv) * vnew_ref[0]


def _swa_decode(q, k_new, v_new, state_kt, state_vt, sink):
    n = q.shape[0]
    buf = state_kt.shape[-1]
    qbd = _block_diag_queries(q)
    q16 = q.reshape(n, N_HEADS, HEAD_DIM)
    knew16 = jnp.repeat(k_new.reshape(n, N_KV_HEADS, HEAD_DIM), GROUP, axis=1)
    per_n = lambda i: (i, 0, 0)
    out = pl.pallas_call(
        _swa_decode_kernel,
        grid=(n,),
        in_specs=[pl.BlockSpec((1, N_HEADS, KV_WIDTH), per_n),
                  pl.BlockSpec((1, N_HEADS, HEAD_DIM), per_n),
                  pl.BlockSpec((1, N_HEADS, HEAD_DIM), per_n),
                  pl.BlockSpec((1, 1, KV_WIDTH), per_n),
                  pl.BlockSpec((N_HEADS, 1), lambda i: (0, 0)),
                  pl.BlockSpec((1, KV_WIDTH, buf), per_n),
                  pl.BlockSpec((1, KV_WIDTH, buf), per_n)],
        out_specs=pl.BlockSpec((1, N_HEADS, KV_WIDTH), per_n),
        out_shape=jax.ShapeDtypeStruct((n, N_HEADS, KV_WIDTH), F32),
        compiler_params=_params("parallel"),
        name="swa_decode",
    )(qbd, q16, knew16, v_new.reshape(n, 1, KV_WIDTH), sink.astype(F32).reshape(N_HEADS, 1),
      state_kt, state_vt)
    return _own_kv_head_columns(out)


def _rope_tables(pos):
    half = HEAD_DIM // 2
    inv = ROPE_THETA ** (-jnp.arange(half, dtype=F32) * 2.0 / HEAD_DIM)
    ang = pos.astype(F32)[:, None] * inv[None, :]
    reps = LANES // half
    return jnp.tile(jnp.cos(ang), (1, reps)), jnp.tile(jnp.sin(ang), (1, reps))


def _with_rotate_half(w, rest):
    d = w.shape[0]
    w4 = w.reshape(d, -1, 2, HEAD_DIM // 2)
    rot = jnp.concatenate([-w4[:, :, 1:2], w4[:, :, 0:1]], axis=2).reshape(d, -1)
    return jnp.concatenate([w, rot, rest], axis=1).astype(BF16)


def kernel(x_prompt, x_sample, cache_a_k, cache_a_v, state_b_k, state_b_v, page_table, w_in_a,
           sb_bias, w_in_b, w_kv_shared, w_out, sinks, ln_g, ln_b):
    b, s, d = x_prompt.shape
    n, dec_t, _ = x_sample.shape
    assert dec_t == 1 and d == N_HEADS * HEAD_DIM
    n_a = w_in_a.shape[0]
    n_b = w_in_b.shape[0]
    depth = w_out.shape[0]
    alpha = (2.0 * depth) ** 0.25
    page = cache_a_k.shape[2]
    past = page_table.shape[1] * page
    buf = state_b_k.shape[1]
    assert buf == WINDOW and s % WINDOW == 0

    xp = x_prompt.reshape(b * s, d)
    xs = x_sample.reshape(n, d)
    w_out_bf = w_out.astype(BF16)
    cache_kt = cache_a_k.transpose(0, 1, 3, 4, 2).reshape(n_a, -1, KV_WIDTH, page)
    cache_vt = cache_a_v.transpose(0, 1, 3, 4, 2).reshape(n_a, -1, KV_WIDTH, page)

    ak_p, av_p, ak_s, av_s = [], [], [], []
    for l in range(n_a):
        wa = w_in_a[l].astype(BF16)
        qp, kp, vp, gp = _proj_a(xp, wa)
        qs, ks, vs, gs = _proj_a(xs, wa)
        op = _sb_prompt(qp, kp, vp, sb_bias[l].astype(F32), b, s).reshape(b * s, d)
        os_ = _sb_decode(qs, cache_kt, cache_vt, l, page_table, sb_bias[l])
        xp = _post(op, gp, xp, w_out_bf[l], ln_g[l], ln_b[l], alpha)
        xs = _post(os_, gs, xs, w_out_bf[l], ln_g[l], ln_b[l], alpha)
        ak_p.append(kp.reshape(b, s, N_KV_HEADS, HEAD_DIM))
        av_p.append(vp.reshape(b, s, N_KV_HEADS, HEAD_DIM))
        ak_s.append(ks.reshape(n, 1, N_KV_HEADS, HEAD_DIM))
        av_s.append(vs.reshape(n, 1, N_KV_HEADS, HEAD_DIM))

    cos_p, sin_p = _rope_tables(jnp.arange(s))
    cos_p, sin_p = jnp.tile(cos_p, (b, 1)), jnp.tile(sin_p, (b, 1))
    cos_s, sin_s = _rope_tables(jnp.full((n,), past))

    w_kv = _with_rotate_half(w_kv_shared[:, :KV_WIDTH], w_kv_shared[:, KV_WIDTH:])
    kp_sh, vp_sh = _proj_kv(xp, w_kv, cos_p, sin_p)
    ks_sh, vs_sh = _proj_kv(xs, w_kv, cos_s, sin_s)
    kh = _to_heads(kp_sh.astype(BF16), b, s, N_KV_HEADS)
    vh = _to_heads(vp_sh.astype(BF16), b, s, N_KV_HEADS)
    state_kt = state_b_k.transpose(0, 2, 3, 1).reshape(n, KV_WIDTH, buf)
    state_vt = state_b_v.transpose(0, 2, 3, 1).reshape(n, KV_WIDTH, buf)

    for j in range(n_b):
        l = n_a + j
        wb = _with_rotate_half(w_in_b[j][:, :d], w_in_b[j][:, d:])
        qp, gp = _proj_b(xp, wb, cos_p, sin_p)
        qs, gs = _proj_b(xs, wb, cos_s, sin_s)
        op = _swa_prompt(qp, kh, vh, sinks[j].astype(F32), b, s)
        os_ = _swa_decode(qs, ks_sh, vs_sh, state_kt, state_vt, sinks[j])
        xp = _post(op, gp, xp, w_out_bf[l], ln_g[l], ln_b[l], alpha)
        xs = _post(os_, gs, xs, w_out_bf[l], ln_g[l], ln_b[l], alpha)

    kp4 = kp_sh.reshape(b, s, N_KV_HEADS, HEAD_DIM)
    vp4 = vp_sh.reshape(b, s, N_KV_HEADS, HEAD_DIM)
    ks4 = ks_sh.reshape(n, 1, N_KV_HEADS, HEAD_DIM)
    vs4 = vs_sh.reshape(n, 1, N_KV_HEADS, HEAD_DIM)
    w_p = min(WINDOW, s)
    return (xp.reshape(b, s, d), xs.reshape(n, 1, d),
            jnp.stack(ak_p), jnp.stack(av_p), jnp.stack(ak_s), jnp.stack(av_s),
            kp4[:, s - w_p:], vp4[:, s - w_p:],
            jnp.concatenate([state_b_k, ks4], axis=1)[:, dec_t:],
            jnp.concatenate([state_b_v, vs4], axis=1)[:, dec_t:])
```

```python
import functools

import numpy as np
import jax
import jax.numpy as jnp
from jax import lax
from jax.experimental import pallas as pl
from jax.experimental.pallas import tpu as pltpu

HEAD_DIM = 64
N_HEADS = 16
N_KV_HEADS = 4
GROUP = N_HEADS // N_KV_HEADS
KV_WIDTH = N_KV_HEADS * HEAD_DIM
WINDOW = 128
ROPE_THETA = 10000.0
LN_EPS = 1e-5
MASK_VALUE = -1e30
QK_SCALE = HEAD_DIM ** -0.5
LOG2E = 1.4426950408889634
SB_AUG = 2 * HEAD_DIM

SUBLANES = 8
LANES = 128
VMEM_LIMIT_BYTES = 48 * 1024 * 1024

BF16 = jnp.bfloat16
F32 = jnp.float32
NT_DIMS = (((1,), (1,)), ((), ()))


def _params(*semantics):
    return pltpu.CompilerParams(dimension_semantics=semantics,
                                vmem_limit_bytes=VMEM_LIMIT_BYTES)


def _row_tile(m):
    for t in (512, 256, 128):
        if m % t == 0:
            return t
    return m


def _softplus(z):
    return jnp.maximum(z, 0.0) + jnp.log(1.0 + jnp.exp(-jnp.abs(z)))


def _store_heads(q_ref, q, aug_ref):
    rows = q.shape[0]
    left = lax.broadcasted_iota(jnp.int32, (rows, LANES), 1) < HEAD_DIM
    for pair in range(N_HEADS // 2):
        slab = q[:, pair * LANES:(pair + 1) * LANES]
        even, odd = 2 * pair, 2 * pair + 1
        q_ref[0, even] = jnp.where(left, slab, aug_ref[even:even + 1, :]).astype(BF16)
        q_ref[0, odd] = jnp.where(left, pltpu.roll(slab, HEAD_DIM, 1),
                                  aug_ref[odd:odd + 1, :]).astype(BF16)


def _head_major_spec(tm, tiles_per_batch):
    return pl.BlockSpec((1, N_HEADS, tm, SB_AUG),
                        lambda i: (i // tiles_per_batch, 0, i % tiles_per_batch, 0))


def _proj_a_kernel(x_ref, w_ref, aug_ref, q_ref, k_ref, v_ref, g_ref):
    xb = x_ref[...].astype(BF16)
    d = g_ref.shape[1]
    kw = k_ref.shape[1]
    dot = functools.partial(jnp.dot, preferred_element_type=F32)
    _store_heads(q_ref, dot(xb, w_ref[:, 0:d]) * QK_SCALE, aug_ref)
    k_ref[...] = dot(xb, w_ref[:, d:d + kw])
    v_ref[...] = dot(xb, w_ref[:, d + kw:d + 2 * kw])
    g_ref[...] = dot(xb, w_ref[:, d + 2 * kw:])


def _proj_a(x, w, aug, b):
    m, d = x.shape
    n = w.shape[1]
    s = m // b
    tm = _row_tile(s)
    row = lambda i: (i, 0)
    const = lambda i: (0, 0)
    return pl.pallas_call(
        _proj_a_kernel,
        grid=(m // tm,),
        in_specs=[pl.BlockSpec((tm, d), row), pl.BlockSpec((d, n), const),
                  pl.BlockSpec((N_HEADS, SB_AUG), const)],
        out_specs=[_head_major_spec(tm, s // tm), pl.BlockSpec((tm, KV_WIDTH), row),
                   pl.BlockSpec((tm, KV_WIDTH), row), pl.BlockSpec((tm, d), row)],
        out_shape=[jax.ShapeDtypeStruct((b, N_HEADS, s, SB_AUG), BF16),
                   jax.ShapeDtypeStruct((m, KV_WIDTH), F32),
                   jax.ShapeDtypeStruct((m, KV_WIDTH), F32), jax.ShapeDtypeStruct((m, d), F32)],
        compiler_params=_params("parallel"),
        name="proj_a",
    )(x, w, aug)


def _proj_b_kernel(x_ref, w_ref, cos_ref, sin_ref, aug_ref, q_ref, g_ref):
    xb = x_ref[...].astype(BF16)
    d = g_ref.shape[1]
    dot = functools.partial(jnp.dot, preferred_element_type=F32)
    reps = d // cos_ref.shape[1]
    cos = jnp.tile(cos_ref[...], (1, reps))
    sin = jnp.tile(sin_ref[...], (1, reps))
    q = dot(xb, w_ref[:, 0:d]) * cos + dot(xb, w_ref[:, d:2 * d]) * sin
    _store_heads(q_ref, q * QK_SCALE, aug_ref)
    g_ref[...] = dot(xb, w_ref[:, 2 * d:])


def _proj_b(x, w, cos, sin, aug, b):
    m, d = x.shape
    n = w.shape[1]
    s = m // b
    tm = _row_tile(s)
    row = lambda i: (i, 0)
    const = lambda i: (0, 0)
    return pl.pallas_call(
        _proj_b_kernel,
        grid=(m // tm,),
        in_specs=[pl.BlockSpec((tm, d), row), pl.BlockSpec((d, n), const),
                  pl.BlockSpec((tm, LANES), row), pl.BlockSpec((tm, LANES), row),
                  pl.BlockSpec((N_HEADS, SB_AUG), const)],
        out_specs=[_head_major_spec(tm, s // tm), pl.BlockSpec((tm, d), row)],
        out_shape=[jax.ShapeDtypeStruct((b, N_HEADS, s, SB_AUG), BF16),
                   jax.ShapeDtypeStruct((m, d), F32)],
        compiler_params=_params("parallel"),
        name="proj_b",
    )(x, w, cos, sin, aug)


def _proj_kv_kernel(x_ref, w_ref, cos_ref, sin_ref, k_ref, v_ref):
    xb = x_ref[...].astype(BF16)
    kw = k_ref.shape[1]
    dot = functools.partial(jnp.dot, preferred_element_type=F32)
    reps = kw // cos_ref.shape[1]
    cos = jnp.tile(cos_ref[...], (1, reps))
    sin = jnp.tile(sin_ref[...], (1, reps))
    k_ref[...] = dot(xb, w_ref[:, 0:kw]) * cos + dot(xb, w_ref[:, kw:2 * kw]) * sin
    v_ref[...] = dot(xb, w_ref[:, 2 * kw:])


def _proj_kv(x, w, cos, sin):
    m, d = x.shape
    n = w.shape[1]
    tm = _row_tile(m)
    row = lambda i: (i, 0)
    return pl.pallas_call(
        _proj_kv_kernel,
        grid=(m // tm,),
        in_specs=[pl.BlockSpec((tm, d), row), pl.BlockSpec((d, n), lambda i: (0, 0)),
                  pl.BlockSpec((tm, LANES), row), pl.BlockSpec((tm, LANES), row)],
        out_specs=[pl.BlockSpec((tm, KV_WIDTH), row), pl.BlockSpec((tm, KV_WIDTH), row)],
        out_shape=[jax.ShapeDtypeStruct((m, KV_WIDTH), F32), jax.ShapeDtypeStruct((m, KV_WIDTH), F32)],
        compiler_params=_params("parallel"),
        name="proj_kv",
    )(x, w, cos, sin)


def _post_kernel(o_ref, g_ref, x_ref, w_ref, lng_ref, lnb_ref, out_ref, *, alpha):
    gate = g_ref[...]
    mixed = (o_ref[...] * (gate * jax.nn.sigmoid(gate))).astype(BF16)
    h = alpha * x_ref[...] + jnp.dot(mixed, w_ref[...], preferred_element_type=F32)
    mu = jnp.mean(h, axis=-1, keepdims=True)
    c = h - mu
    var = jnp.mean(c * c, axis=-1, keepdims=True)
    out_ref[...] = c * lax.rsqrt(var + LN_EPS) * lng_ref[...] + lnb_ref[...]


def _post(o, gate, x, w, ln_g, ln_b, alpha):
    m, d = x.shape
    tm = _row_tile(m)
    row = lambda i: (i, 0)
    const = lambda i: (0, 0)
    return pl.pallas_call(
        functools.partial(_post_kernel, alpha=alpha),
        grid=(m // tm,),
        in_specs=[pl.BlockSpec((tm, d), row), pl.BlockSpec((tm, d), row), pl.BlockSpec((tm, d), row),
                  pl.BlockSpec((d, d), const), pl.BlockSpec((1, d), const), pl.BlockSpec((1, d), const)],
        out_specs=pl.BlockSpec((tm, d), row),
        out_shape=jax.ShapeDtypeStruct((m, d), F32),
        compiler_params=_params("parallel"),
        name="post",
    )(o, gate, x, w, ln_g.reshape(1, d), ln_b.reshape(1, d))


SB_BLOCK = 256


def _sb_prompt_kernel(q_ref, k_ref, vt_ref, mask_ref, o_ref, acc_ref, carry_ref, z_ref, p_ref, *,
                      blk):
    i = pl.program_id(2)
    n_groups = blk // SUBLANES
    width = GROUP * blk
    acc_ref[...] = jnp.zeros_like(acc_ref)
    carry_ref[...] = jnp.zeros_like(carry_ref)
    sub = lax.broadcasted_iota(jnp.int32, (SUBLANES, width), 0)
    q4 = q_ref[0].reshape(width, SB_AUG)

    def logits(jj):
        start = pl.multiple_of(jj * blk, blk)
        kb = k_ref[0, 0, pl.ds(start, blk), :]
        z_ref[...] = lax.dot_general(kb, q4, NT_DIMS, preferred_element_type=F32)

    def kv_block(jj, masked):
        start = pl.multiple_of(jj * blk, blk)
        vb = vt_ref[0, 0, :, pl.ds(start, blk)]
        run = jnp.zeros((SUBLANES, width), F32)
        for t in reversed(range(n_groups)):
            rows = slice(t * SUBLANES, (t + 1) * SUBLANES)
            zt = z_ref[rows, :]
            if masked:
                zt = zt + jnp.tile(mask_ref[rows, :], (1, GROUP))
            e = jnp.exp2(jnp.abs(zt) * (-LOG2E))
            run = run + (jnp.maximum(zt, 0.0) + jnp.log(1.0 + e))
            p_ref[rows, :] = zt - run
        logits(jnp.maximum(jj - 1, 0))
        suffix = run
        for sh in (1, 2, 4):
            rolled = pltpu.roll(suffix, SUBLANES - sh, 0)
            suffix = suffix + jnp.where(sub < SUBLANES - sh, rolled, 0.0)
        carry = carry_ref[...]
        off = (carry + (suffix - run)) * LOG2E
        carry_ref[...] = carry + jnp.broadcast_to(suffix[0:1], (SUBLANES, width))
        for r in range(GROUP):
            cols = slice(r * blk, (r + 1) * blk)
            off_r = off[:, cols]
            a = jnp.concatenate(
                [jnp.exp2(p_ref[t * SUBLANES:(t + 1) * SUBLANES, cols] * LOG2E - off_r)
                 for t in range(n_groups)], axis=0).astype(BF16)
            acc_ref[r] += jnp.dot(vb, a, preferred_element_type=F32)

    logits(i)
    kv_block(i, True)

    def earlier(t, c):
        kv_block(i - 1 - t, False)
        return c

    lax.fori_loop(0, i, earlier, 0)
    o_ref[0] = acc_ref[...].reshape(GROUP * HEAD_DIM, blk).T


def _sb_block_size(s):
    return SB_BLOCK if s % SB_BLOCK == 0 else LANES


def _sb_diag_mask(blk):
    rows = np.arange(blk)
    key = (rows % SUBLANES) * (blk // SUBLANES) + rows // SUBLANES
    return np.where(key[:, None] < np.arange(blk)[None, :], 0.0, MASK_VALUE).astype(np.float32)


def _sb_aug_rows(bias):
    bias3 = jnp.stack([p.astype(F32) for p in _split3(bias.astype(F32))], axis=-1)
    return jnp.pad(bias3, ((0, 0), (HEAD_DIM, SB_AUG - HEAD_DIM - 3)))


def _sb_prompt(q_aug, k, v, b, s):
    blk = _sb_block_size(s)
    nb = s // blk
    ng = blk // SUBLANES
    pad = SB_AUG - HEAD_DIM - 3
    k6 = k.astype(BF16).reshape(b, nb, SUBLANES, ng, N_KV_HEADS, HEAD_DIM)
    kperm = k6.transpose(0, 4, 1, 3, 2, 5).reshape(b, N_KV_HEADS, s, HEAD_DIM)
    ones_cols = jnp.pad(jnp.ones((3,), BF16), (0, pad))
    k_aug = jnp.concatenate(
        [kperm, jnp.broadcast_to(ones_cols, (b, N_KV_HEADS, s, SB_AUG - HEAD_DIM))], axis=-1)
    v6 = v.astype(BF16).reshape(b, nb, SUBLANES, ng, N_KV_HEADS, HEAD_DIM)
    vt = v6.transpose(0, 4, 5, 1, 3, 2).reshape(b, N_KV_HEADS, HEAD_DIM, s)
    mask = jnp.asarray(_sb_diag_mask(blk))
    return pl.pallas_call(
        functools.partial(_sb_prompt_kernel, blk=blk),
        grid=(b, N_KV_HEADS, nb),
        in_specs=[
            pl.BlockSpec((1, GROUP, blk, SB_AUG), lambda bi, g, i: (bi, g, i, 0)),
            pl.BlockSpec((1, 1, s, SB_AUG), lambda bi, g, i: (bi, g, 0, 0)),
            pl.BlockSpec((1, 1, HEAD_DIM, s), lambda bi, g, i: (bi, g, 0, 0)),
            pl.BlockSpec((blk, blk), lambda bi, g, i: (0, 0)),
        ],
        out_specs=pl.BlockSpec((1, blk, GROUP * HEAD_DIM), lambda bi, g, i: (bi, i, g)),
        out_shape=jax.ShapeDtypeStruct((b, s, N_HEADS * HEAD_DIM), F32),
        scratch_shapes=[pltpu.VMEM((GROUP, HEAD_DIM, blk), F32),
                        pltpu.VMEM((SUBLANES, GROUP * blk), F32),
                        pltpu.VMEM((blk, GROUP * blk), F32),
                        pltpu.VMEM((blk, GROUP * blk), F32)],
        compiler_params=_params("parallel", "parallel", "arbitrary"),
        name="sb_prompt",
    )(q_aug, k_aug, vt, mask)


SB_DECODE_PAGES = 32


def _flat_queries(q_aug):
    n = q_aug.shape[2]
    return q_aug[0, :, :, :HEAD_DIM].transpose(1, 0, 2).reshape(n, N_HEADS * HEAD_DIM)


def _block_diag_queries(q):
    n = q.shape[0]
    q4 = q.reshape(n, N_HEADS, 1, HEAD_DIM)
    sel = (np.arange(N_HEADS)[:, None] // GROUP == np.arange(N_KV_HEADS)[None, :])
    sel = jnp.asarray(sel.reshape(1, N_HEADS, N_KV_HEADS, 1), q.dtype)
    return (q4 * sel).reshape(n, N_HEADS, KV_WIDTH)


def _own_kv_head_columns(o):
    n = o.shape[0]
    o5 = o.reshape(n, N_KV_HEADS, GROUP, N_KV_HEADS, HEAD_DIM)
    own = jnp.stack([o5[:, g, :, g, :] for g in range(N_KV_HEADS)], axis=1)
    return own.reshape(n, N_HEADS * HEAD_DIM)


def _split3(x):
    hi = x.astype(BF16)
    r1 = x - hi.astype(F32)
    mid = r1.astype(BF16)
    lo = (r1 - mid.astype(F32)).astype(BF16)
    return hi, mid, lo


def _sb_decode_kernel(pt_ref, q_ref, bias_ref, tri_ref, *refs, pages):
    k_refs = refs[:pages]
    v_refs = refs[pages:2 * pages]
    o_ref, carry_ref, acc_ref = refs[2 * pages:]
    c = pl.program_id(1)

    @pl.when(c == 0)
    def _():
        carry_ref[...] = jnp.zeros_like(carry_ref)
        acc_ref[...] = jnp.zeros_like(acc_ref)

    qbd = q_ref[0]
    tri = tri_ref[...]
    dot = functools.partial(jnp.dot, preferred_element_type=F32)
    z = jnp.concatenate([dot(qbd, k_refs[j][0, 0].astype(BF16)) for j in range(pages)], axis=0)
    z = z + jnp.tile(bias_ref[...], (pages, 1))
    sp = _softplus(z)
    hi, mid, lo = _split3(sp)
    suffix = dot(hi, tri) + dot(mid, tri) + dot(lo, tri)
    logw = z - suffix
    totals = jnp.broadcast_to(suffix[:, 0:1], suffix.shape)
    carry = carry_ref[...]
    acc = acc_ref[...]
    for j in reversed(range(pages)):
        rows = slice(j * N_HEADS, (j + 1) * N_HEADS)
        a = jnp.exp(logw[rows] - carry)
        carry = carry + totals[rows]
        acc = acc + lax.dot_general(a.astype(BF16), v_refs[j][0, 0].astype(BF16), NT_DIMS,
                                    preferred_element_type=F32)
    carry_ref[...] = carry
    acc_ref[...] = acc

    @pl.when(c == pl.num_programs(1) - 1)
    def _():
        o_ref[0] = acc


def _sb_decode(q, cache_kt, cache_vt, layer, page_table, bias):
    n = q.shape[0]
    n_pages = page_table.shape[1]
    page = cache_kt.shape[-1]
    pages = SB_DECODE_PAGES if n_pages % SB_DECODE_PAGES == 0 else n_pages
    n_chunks = n_pages // pages
    qbd = _block_diag_queries(q)
    bias_b = jnp.broadcast_to(bias.astype(F32)[:, None], (N_HEADS, page))
    tri = jnp.asarray(np.tril(np.ones((page, page), np.float32)), BF16)

    def page_spec(j):
        def index(bi, c, pt):
            return (layer, pt[bi, (n_chunks - 1 - c) * pages + j], 0, 0)
        return pl.BlockSpec((1, 1, KV_WIDTH, page), index)

    grid_spec = pltpu.PrefetchScalarGridSpec(
        num_scalar_prefetch=1,
        grid=(n, n_chunks),
        in_specs=[pl.BlockSpec((1, N_HEADS, KV_WIDTH), lambda bi, c, pt: (bi, 0, 0)),
                  pl.BlockSpec((N_HEADS, page), lambda bi, c, pt: (0, 0)),
                  pl.BlockSpec((page, page), lambda bi, c, pt: (0, 0))]
                 + [page_spec(j) for j in range(pages)] * 2,
        out_specs=pl.BlockSpec((1, N_HEADS, KV_WIDTH), lambda bi, c, pt: (bi, 0, 0)),
        scratch_shapes=[pltpu.VMEM((N_HEADS, page), F32), pltpu.VMEM((N_HEADS, KV_WIDTH), F32)],
    )
    out = pl.pallas_call(
        functools.partial(_sb_decode_kernel, pages=pages),
        grid_spec=grid_spec,
        out_shape=jax.ShapeDtypeStruct((n, N_HEADS, KV_WIDTH), F32),
        compiler_params=_params("parallel", "arbitrary"),
        name="sb_decode",
    )(page_table, qbd, bias_b, tri, *([cache_kt] * pages), *([cache_vt] * pages))
    return _own_kv_head_columns(out)


SWA_Q_TILE = 2 * WINDOW


def _swa_prompt_kernel(sink_ref, q_ref, kp_ref, kc_ref, vp_ref, vc_ref, o_ref):
    g = pl.program_id(1)
    n = pl.program_id(2)
    blk = WINDOW
    sub_blocks = kc_ref.shape[2] // blk
    kband = jnp.concatenate([kp_ref[0, 0], kc_ref[0, 0]], axis=0)
    vband = jnp.concatenate([vp_ref[0, 0], vc_ref[0, 0]], axis=1)
    sk = lax.broadcasted_iota(jnp.int32, (2 * blk, blk), 0)
    t = lax.broadcasted_iota(jnp.int32, (2 * blk, blk), 1)
    diff = t + blk - sk
    in_window = (diff >= 0) & (diff <= WINDOW)
    sink = jnp.concatenate([jnp.full((1, blk), sink_ref[g * GROUP + r], F32) for r in range(GROUP)],
                           axis=1)
    for u in range(sub_blocks):
        q4 = q_ref[0, :, u * blk:(u + 1) * blk, :].reshape(GROUP * blk, q_ref.shape[3])
        kb = kband[u * blk:(u + 2) * blk]
        sc = lax.dot_general(kb, q4, NT_DIMS, preferred_element_type=F32)
        valid = in_window & ((sk >= blk) | (n * sub_blocks + u > 0))
        sc = jnp.where(jnp.tile(valid, (1, GROUP)), sc, MASK_VALUE)
        m = jnp.maximum(jnp.max(sc, axis=0, keepdims=True), sink)
        p = jnp.exp(sc - m)
        den = jnp.sum(p, axis=0, keepdims=True) + jnp.exp(sink - m)
        w = (p * (1.0 / den)).astype(BF16)
        ot = jnp.dot(vband[:, u * blk:(u + 2) * blk], w, preferred_element_type=F32)
        stacked = jnp.concatenate([ot[:, r * blk:(r + 1) * blk] for r in range(GROUP)], axis=0)
        o_ref[0, u * blk:(u + 1) * blk, :] = stacked.T


def _to_heads(x, b, s, heads):
    return x.reshape(b, s, heads, HEAD_DIM).transpose(0, 2, 1, 3)


def _swa_prompt(q_aug, kh, vt, sink, b, s):
    tile = SWA_Q_TILE if s % SWA_Q_TILE == 0 else WINDOW
    per_tile = tile // WINDOW
    prev = lambda bi, g, n: (bi, g, jnp.maximum(n * per_tile - 1, 0), 0)
    prev_t = lambda bi, g, n: (bi, g, 0, jnp.maximum(n * per_tile - 1, 0))
    out = pl.pallas_call(
        _swa_prompt_kernel,
        grid=(b, N_KV_HEADS, s // tile),
        in_specs=[pl.BlockSpec(memory_space=pltpu.SMEM),
                  pl.BlockSpec((1, GROUP, tile, SB_AUG), lambda bi, g, n: (bi, g, n, 0)),
                  pl.BlockSpec((1, 1, WINDOW, SB_AUG), prev),
                  pl.BlockSpec((1, 1, tile, SB_AUG), lambda bi, g, n: (bi, g, n, 0)),
                  pl.BlockSpec((1, 1, HEAD_DIM, WINDOW), prev_t),
                  pl.BlockSpec((1, 1, HEAD_DIM, tile), lambda bi, g, n: (bi, g, 0, n))],
        out_specs=pl.BlockSpec((1, tile, GROUP * HEAD_DIM), lambda bi, g, n: (bi, n, g)),
        out_shape=jax.ShapeDtypeStruct((b, s, N_HEADS * HEAD_DIM), F32),
        compiler_params=_params("parallel", "parallel", "parallel"),
        name="swa_prompt",
    )(sink, q_aug, kh, kh, vt, vt)
    return out.reshape(b * s, N_HEADS * HEAD_DIM)


def _swa_decode_kernel(qbd_ref, q_ref, knew_ref, vnew_ref, sink_ref, kt_ref, vt_ref, o_ref):
    z = jnp.dot(qbd_ref[0], kt_ref[0].astype(BF16), preferred_element_type=F32)
    zn = jnp.sum(q_ref[0].astype(F32) * knew_ref[0], axis=-1, keepdims=True)
    sink = sink_ref[...]
    m = jnp.maximum(jnp.maximum(jnp.max(z, axis=-1, keepdims=True), zn), sink)
    p = jnp.exp(z - m)
    pn = jnp.exp(zn - m)
    inv = 1.0 / (jnp.sum(p, axis=-1, keepdims=True) + pn + jnp.exp(sink - m))
    out = lax.dot_general((p * inv).astype(BF16), vt_ref[0].astype(BF16), NT_DIMS,
                          preferred_element_type=F32)
    o_ref[0] = out + (pn * inv) * vnew_ref[0]


def _swa_decode(q, k_new, v_new, state_kt, state_vt, sink):
    n = q.shape[0]
    buf = state_kt.shape[-1]
    qbd = _block_diag_queries(q)
    q16 = q.reshape(n, N_HEADS, HEAD_DIM)
    knew16 = jnp.repeat(k_new.reshape(n, N_KV_HEADS, HEAD_DIM), GROUP, axis=1)
    per_n = lambda i: (i, 0, 0)
    out = pl.pallas_call(
        _swa_decode_kernel,
        grid=(n,),
        in_specs=[pl.BlockSpec((1, N_HEADS, KV_WIDTH), per_n),
                  pl.BlockSpec((1, N_HEADS, HEAD_DIM), per_n),
                  pl.BlockSpec((1, N_HEADS, HEAD_DIM), per_n),
                  pl.BlockSpec((1, 1, KV_WIDTH), per_n),
                  pl.BlockSpec((N_HEADS, 1), lambda i: (0, 0)),
                  pl.BlockSpec((1, KV_WIDTH, buf), per_n),
                  pl.BlockSpec((1, KV_WIDTH, buf), per_n)],
        out_specs=pl.BlockSpec((1, N_HEADS, KV_WIDTH), per_n),
        out_shape=jax.ShapeDtypeStruct((n, N_HEADS, KV_WIDTH), F32),
        compiler_params=_params("parallel"),
        name="swa_decode",
    )(qbd, q16, knew16, v_new.reshape(n, 1, KV_WIDTH), sink.astype(F32).reshape(N_HEADS, 1),
      state_kt, state_vt)
    return _own_kv_head_columns(out)


def _rope_tables(pos):
    half = HEAD_DIM // 2
    inv = ROPE_THETA ** (-jnp.arange(half, dtype=F32) * 2.0 / HEAD_DIM)
    ang = pos.astype(F32)[:, None] * inv[None, :]
    reps = LANES // half
    return jnp.tile(jnp.cos(ang), (1, reps)), jnp.tile(jnp.sin(ang), (1, reps))


def _with_rotate_half(w, rest):
    d = w.shape[0]
    w4 = w.reshape(d, -1, 2, HEAD_DIM // 2)
    rot = jnp.concatenate([-w4[:, :, 1:2], w4[:, :, 0:1]], axis=2).reshape(d, -1)
    return jnp.concatenate([w, rot, rest], axis=1).astype(BF16)


def kernel(x_prompt, x_sample, cache_a_k, cache_a_v, state_b_k, state_b_v, page_table, w_in_a,
           sb_bias, w_in_b, w_kv_shared, w_out, sinks, ln_g, ln_b):
    b, s, d = x_prompt.shape
    n, dec_t, _ = x_sample.shape
    assert dec_t == 1 and d == N_HEADS * HEAD_DIM
    n_a = w_in_a.shape[0]
    n_b = w_in_b.shape[0]
    depth = w_out.shape[0]
    alpha = (2.0 * depth) ** 0.25
    page = cache_a_k.shape[2]
    past = page_table.shape[1] * page
    buf = state_b_k.shape[1]
    assert buf == WINDOW and s % WINDOW == 0

    xp = x_prompt.reshape(b * s, d)
    xs = x_sample.reshape(n, d)
    w_out_bf = w_out.astype(BF16)
    cache_kt = cache_a_k.transpose(0, 1, 3, 4, 2).reshape(n_a, -1, KV_WIDTH, page)
    cache_vt = cache_a_v.transpose(0, 1, 3, 4, 2).reshape(n_a, -1, KV_WIDTH, page)

    ak_p, av_p, ak_s, av_s = [], [], [], []
    for l in range(n_a):
        wa = w_in_a[l].astype(BF16)
        aug = _sb_aug_rows(sb_bias[l])
        qp, kp, vp, gp = _proj_a(xp, wa, aug, b)
        qs, ks, vs, gs = _proj_a(xs, wa, aug, 1)
        op = _sb_prompt(qp, kp, vp, b, s).reshape(b * s, d)
        os_ = _sb_decode(_flat_queries(qs), cache_kt, cache_vt, l, page_table, sb_bias[l])
        xp = _post(op, gp, xp, w_out_bf[l], ln_g[l], ln_b[l], alpha)
        xs = _post(os_, gs, xs, w_out_bf[l], ln_g[l], ln_b[l], alpha)
        ak_p.append(kp.reshape(b, s, N_KV_HEADS, HEAD_DIM))
        av_p.append(vp.reshape(b, s, N_KV_HEADS, HEAD_DIM))
        ak_s.append(ks.reshape(n, 1, N_KV_HEADS, HEAD_DIM))
        av_s.append(vs.reshape(n, 1, N_KV_HEADS, HEAD_DIM))

    cos_p, sin_p = _rope_tables(jnp.arange(s))
    cos_p, sin_p = jnp.tile(cos_p, (b, 1)), jnp.tile(sin_p, (b, 1))
    cos_s, sin_s = _rope_tables(jnp.full((n,), past))

    w_kv = _with_rotate_half(w_kv_shared[:, :KV_WIDTH], w_kv_shared[:, KV_WIDTH:])
    kp_sh, vp_sh = _proj_kv(xp, w_kv, cos_p, sin_p)
    ks_sh, vs_sh = _proj_kv(xs, w_kv, cos_s, sin_s)
    kh = jnp.pad(_to_heads(kp_sh.astype(BF16), b, s, N_KV_HEADS),
                 ((0, 0), (0, 0), (0, 0), (0, SB_AUG - HEAD_DIM)))
    vt = vp_sh.astype(BF16).reshape(b, s, N_KV_HEADS, HEAD_DIM).transpose(0, 2, 3, 1)
    no_aug = jnp.zeros((N_HEADS, SB_AUG), F32)
    state_kt = state_b_k.transpose(0, 2, 3, 1).reshape(n, KV_WIDTH, buf)
    state_vt = state_b_v.transpose(0, 2, 3, 1).reshape(n, KV_WIDTH, buf)

    for j in range(n_b):
        l = n_a + j
        wb = _with_rotate_half(w_in_b[j][:, :d], w_in_b[j][:, d:])
        qp, gp = _proj_b(xp, wb, cos_p, sin_p, no_aug, b)
        qs, gs = _proj_b(xs, wb, cos_s, sin_s, no_aug, 1)
        op = _swa_prompt(qp, kh, vt, sinks[j].astype(F32), b, s)
        os_ = _swa_decode(_flat_queries(qs), ks_sh, vs_sh, state_kt, state_vt, sinks[j])
        xp = _post(op, gp, xp, w_out_bf[l], ln_g[l], ln_b[l], alpha)
        xs = _post(os_, gs, xs, w_out_bf[l], ln_g[l], ln_b[l], alpha)

    kp4 = kp_sh.reshape(b, s, N_KV_HEADS, HEAD_DIM)
    vp4 = vp_sh.reshape(b, s, N_KV_HEADS, HEAD_DIM)
    ks4 = ks_sh.reshape(n, 1, N_KV_HEADS, HEAD_DIM)
    vs4 = vs_sh.reshape(n, 1, N_KV_HEADS, HEAD_DIM)
    w_p = min(WINDOW, s)
    return (xp.reshape(b, s, d), xs.reshape(n, 1, d),
            jnp.stack(ak_p), jnp.stack(av_p), jnp.stack(ak_s), jnp.stack(av_s),
            kp4[:, s - w_p:], vp4[:, s - w_p:],
            jnp.concatenate([state_b_k, ks4], axis=1)[:, dec_t:],
            jnp.concatenate([state_b_v, vs4], axis=1)[:, dec_t:])
```

```python
import functools

import numpy as np
import jax
import jax.numpy as jnp
from jax import lax
from jax.experimental import pallas as pl
from jax.experimental.pallas import tpu as pltpu

HEAD_DIM = 64
N_HEADS = 16
N_KV_HEADS = 4
GROUP = N_HEADS // N_KV_HEADS
KV_WIDTH = N_KV_HEADS * HEAD_DIM
WINDOW = 128
ROPE_THETA = 10000.0
LN_EPS = 1e-5
MASK_VALUE = -1e30
QK_SCALE = HEAD_DIM ** -0.5
LOG2E = 1.4426950408889634
SB_AUG = 2 * HEAD_DIM

SUBLANES = 8
LANES = 128
VMEM_LIMIT_BYTES = 48 * 1024 * 1024

BF16 = jnp.bfloat16
F32 = jnp.float32
NT_DIMS = (((1,), (1,)), ((), ()))


def _params(*semantics):
    return pltpu.CompilerParams(dimension_semantics=semantics,
                                vmem_limit_bytes=VMEM_LIMIT_BYTES)


def _row_tile(m):
    for t in (512, 256, 128):
        if m % t == 0:
            return t
    return m


def _softplus(z):
    return jnp.maximum(z, 0.0) + jnp.log(1.0 + jnp.exp(-jnp.abs(z)))


def _store_heads(q_ref, q, aug_ref):
    rows = q.shape[0]
    left = lax.broadcasted_iota(jnp.int32, (rows, LANES), 1) < HEAD_DIM
    for pair in range(N_HEADS // 2):
        slab = q[:, pair * LANES:(pair + 1) * LANES]
        even, odd = 2 * pair, 2 * pair + 1
        q_ref[0, even] = jnp.where(left, slab, aug_ref[even:even + 1, :]).astype(BF16)
        q_ref[0, odd] = jnp.where(left, pltpu.roll(slab, HEAD_DIM, 1),
                                  aug_ref[odd:odd + 1, :]).astype(BF16)


def _head_major_spec(tm, tiles_per_batch):
    return pl.BlockSpec((1, N_HEADS, tm, SB_AUG),
                        lambda i: (i // tiles_per_batch, 0, i % tiles_per_batch, 0))


def _proj_a_kernel(x_ref, w_ref, aug_ref, q_ref, k_ref, v_ref, g_ref):
    xb = x_ref[...].astype(BF16)
    d = g_ref.shape[1]
    kw = k_ref.shape[1]
    dot = functools.partial(jnp.dot, preferred_element_type=F32)
    _store_heads(q_ref, dot(xb, w_ref[:, 0:d]) * QK_SCALE, aug_ref)
    k_ref[...] = dot(xb, w_ref[:, d:d + kw])
    v_ref[...] = dot(xb, w_ref[:, d + kw:d + 2 * kw])
    g_ref[...] = dot(xb, w_ref[:, d + 2 * kw:])


def _proj_a(x, w, aug, b):
    m, d = x.shape
    n = w.shape[1]
    s = m // b
    tm = _row_tile(s)
    row = lambda i: (i, 0)
    const = lambda i: (0, 0)
    return pl.pallas_call(
        _proj_a_kernel,
        grid=(m // tm,),
        in_specs=[pl.BlockSpec((tm, d), row), pl.BlockSpec((d, n), const),
                  pl.BlockSpec((N_HEADS, SB_AUG), const)],
        out_specs=[_head_major_spec(tm, s // tm), pl.BlockSpec((tm, KV_WIDTH), row),
                   pl.BlockSpec((tm, KV_WIDTH), row), pl.BlockSpec((tm, d), row)],
        out_shape=[jax.ShapeDtypeStruct((b, N_HEADS, s, SB_AUG), BF16),
                   jax.ShapeDtypeStruct((m, KV_WIDTH), F32),
                   jax.ShapeDtypeStruct((m, KV_WIDTH), F32), jax.ShapeDtypeStruct((m, d), F32)],
        compiler_params=_params("parallel"),
        name="proj_a",
    )(x, w, aug)


def _proj_b_kernel(x_ref, w_ref, cos_ref, sin_ref, aug_ref, q_ref, g_ref):
    xb = x_ref[...].astype(BF16)
    d = g_ref.shape[1]
    dot = functools.partial(jnp.dot, preferred_element_type=F32)
    reps = d // cos_ref.shape[1]
    cos = jnp.tile(cos_ref[...], (1, reps))
    sin = jnp.tile(sin_ref[...], (1, reps))
    q = dot(xb, w_ref[:, 0:d]) * cos + dot(xb, w_ref[:, d:2 * d]) * sin
    _store_heads(q_ref, q * QK_SCALE, aug_ref)
    g_ref[...] = dot(xb, w_ref[:, 2 * d:])


def _proj_b(x, w, cos, sin, aug, b):
    m, d = x.shape
    n = w.shape[1]
    s = m // b
    tm = _row_tile(s)
    row = lambda i: (i, 0)
    const = lambda i: (0, 0)
    return pl.pallas_call(
        _proj_b_kernel,
        grid=(m // tm,),
        in_specs=[pl.BlockSpec((tm, d), row), pl.BlockSpec((d, n), const),
                  pl.BlockSpec((tm, LANES), row), pl.BlockSpec((tm, LANES), row),
                  pl.BlockSpec((N_HEADS, SB_AUG), const)],
        out_specs=[_head_major_spec(tm, s // tm), pl.BlockSpec((tm, d), row)],
        out_shape=[jax.ShapeDtypeStruct((b, N_HEADS, s, SB_AUG), BF16),
                   jax.ShapeDtypeStruct((m, d), F32)],
        compiler_params=_params("parallel"),
        name="proj_b",
    )(x, w, cos, sin, aug)


def _proj_kv_kernel(x_ref, w_ref, cos_ref, sin_ref, k_ref, v_ref):
    xb = x_ref[...].astype(BF16)
    kw = k_ref.shape[1]
    dot = functools.partial(jnp.dot, preferred_element_type=F32)
    reps = kw // cos_ref.shape[1]
    cos = jnp.tile(cos_ref[...], (1, reps))
    sin = jnp.tile(sin_ref[...], (1, reps))
    k_ref[...] = dot(xb, w_ref[:, 0:kw]) * cos + dot(xb, w_ref[:, kw:2 * kw]) * sin
    v_ref[...] = dot(xb, w_ref[:, 2 * kw:])


def _proj_kv(x, w, cos, sin):
    m, d = x.shape
    n = w.shape[1]
    tm = _row_tile(m)
    row = lambda i: (i, 0)
    return pl.pallas_call(
        _proj_kv_kernel,
        grid=(m // tm,),
        in_specs=[pl.BlockSpec((tm, d), row), pl.BlockSpec((d, n), lambda i: (0, 0)),
                  pl.BlockSpec((tm, LANES), row), pl.BlockSpec((tm, LANES), row)],
        out_specs=[pl.BlockSpec((tm, KV_WIDTH), row), pl.BlockSpec((tm, KV_WIDTH), row)],
        out_shape=[jax.ShapeDtypeStruct((m, KV_WIDTH), F32), jax.ShapeDtypeStruct((m, KV_WIDTH), F32)],
        compiler_params=_params("parallel"),
        name="proj_kv",
    )(x, w, cos, sin)


def _post_kernel(o_ref, g_ref, x_ref, w_ref, lng_ref, lnb_ref, out_ref, *, alpha):
    gate = g_ref[...]
    mixed = (o_ref[...] * (gate * jax.nn.sigmoid(gate))).astype(BF16)
    h = alpha * x_ref[...] + jnp.dot(mixed, w_ref[...], preferred_element_type=F32)
    mu = jnp.mean(h, axis=-1, keepdims=True)
    c = h - mu
    var = jnp.mean(c * c, axis=-1, keepdims=True)
    out_ref[...] = c * lax.rsqrt(var + LN_EPS) * lng_ref[...] + lnb_ref[...]


def _post(o, gate, x, w, ln_g, ln_b, alpha):
    m, d = x.shape
    tm = _row_tile(m)
    row = lambda i: (i, 0)
    const = lambda i: (0, 0)
    return pl.pallas_call(
        functools.partial(_post_kernel, alpha=alpha),
        grid=(m // tm,),
        in_specs=[pl.BlockSpec((tm, d), row), pl.BlockSpec((tm, d), row), pl.BlockSpec((tm, d), row),
                  pl.BlockSpec((d, d), const), pl.BlockSpec((1, d), const), pl.BlockSpec((1, d), const)],
        out_specs=pl.BlockSpec((tm, d), row),
        out_shape=jax.ShapeDtypeStruct((m, d), F32),
        compiler_params=_params("parallel"),
        name="post",
    )(o, gate, x, w, ln_g.reshape(1, d), ln_b.reshape(1, d))


SB_BLOCK = 256
SB_UNROLL = 4


def _sb_prompt_kernel(q_ref, k_ref, vt_ref, mask_ref, o_ref, acc_ref, carry_ref, z_ref, p_ref, *,
                      blk):
    i = pl.program_id(2)
    n_groups = blk // SUBLANES
    width = GROUP * blk
    acc_ref[...] = jnp.zeros_like(acc_ref)
    carry_ref[...] = jnp.zeros_like(carry_ref)
    sub = lax.broadcasted_iota(jnp.int32, (SUBLANES, width), 0)
    q4 = q_ref[0].reshape(width, SB_AUG)

    def logits(jj):
        start = pl.multiple_of(jj * blk, blk)
        kb = k_ref[0, 0, pl.ds(start, blk), :]
        z_ref[...] = lax.dot_general(kb, q4, NT_DIMS, preferred_element_type=F32)

    def kv_block(jj, masked):
        start = pl.multiple_of(jj * blk, blk)
        vb = vt_ref[0, 0, :, pl.ds(start, blk)]
        run = jnp.zeros((SUBLANES, width), F32)
        for t in reversed(range(n_groups)):
            rows = slice(t * SUBLANES, (t + 1) * SUBLANES)
            zt = z_ref[rows, :]
            if masked:
                zt = zt + jnp.tile(mask_ref[rows, :], (1, GROUP))
            e = jnp.exp2(jnp.abs(zt) * (-LOG2E))
            run = run + (jnp.maximum(zt, 0.0) + jnp.log(1.0 + e))
            p_ref[rows, :] = zt - run
        logits(jnp.maximum(jj - 1, 0))
        suffix = run
        for sh in (1, 2, 4):
            rolled = pltpu.roll(suffix, SUBLANES - sh, 0)
            suffix = suffix + jnp.where(sub < SUBLANES - sh, rolled, 0.0)
        carry = carry_ref[...]
        off = (carry + (suffix - run)) * LOG2E
        carry_ref[...] = carry + jnp.broadcast_to(suffix[0:1], (SUBLANES, width))
        for r in range(GROUP):
            cols = slice(r * blk, (r + 1) * blk)
            off_r = off[:, cols]
            a = jnp.concatenate(
                [jnp.exp2(p_ref[t * SUBLANES:(t + 1) * SUBLANES, cols] * LOG2E - off_r)
                 for t in range(n_groups)], axis=0).astype(BF16)
            acc_ref[r] += jnp.dot(vb, a, preferred_element_type=F32)

    logits(i)
    kv_block(i, True)

    def earlier_group(t, c):
        jj = i - 1 - SB_UNROLL * t
        for u in range(SB_UNROLL):
            kv_block(jj - u, False)
        return c

    groups = i // SB_UNROLL
    lax.fori_loop(0, groups, earlier_group, 0)

    def earlier_single(t, c):
        kv_block(i - 1 - SB_UNROLL * groups - t, False)
        return c

    lax.fori_loop(0, i % SB_UNROLL, earlier_single, 0)

    o_ref[0] = acc_ref[...].reshape(GROUP * HEAD_DIM, blk).T


def _sb_block_size(s):
    return SB_BLOCK if s % SB_BLOCK == 0 else LANES


def _sb_diag_mask(blk):
    rows = np.arange(blk)
    key = (rows % SUBLANES) * (blk // SUBLANES) + rows // SUBLANES
    return np.where(key[:, None] < np.arange(blk)[None, :], 0.0, MASK_VALUE).astype(np.float32)


def _sb_aug_rows(bias):
    bias3 = jnp.stack([p.astype(F32) for p in _split3(bias.astype(F32))], axis=-1)
    return jnp.pad(bias3, ((0, 0), (HEAD_DIM, SB_AUG - HEAD_DIM - 3)))


def _sb_prompt(q_aug, k, v, b, s):
    blk = _sb_block_size(s)
    nb = s // blk
    ng = blk // SUBLANES
    pad = SB_AUG - HEAD_DIM - 3
    k6 = k.astype(BF16).reshape(b, nb, SUBLANES, ng, N_KV_HEADS, HEAD_DIM)
    kperm = k6.transpose(0, 4, 1, 3, 2, 5).reshape(b, N_KV_HEADS, s, HEAD_DIM)
    ones_cols = jnp.pad(jnp.ones((3,), BF16), (0, pad))
    k_aug = jnp.concatenate(
        [kperm, jnp.broadcast_to(ones_cols, (b, N_KV_HEADS, s, SB_AUG - HEAD_DIM))], axis=-1)
    v6 = v.astype(BF16).reshape(b, nb, SUBLANES, ng, N_KV_HEADS, HEAD_DIM)
    vt = v6.transpose(0, 4, 5, 1, 3, 2).reshape(b, N_KV_HEADS, HEAD_DIM, s)
    mask = jnp.asarray(_sb_diag_mask(blk))
    return pl.pallas_call(
        functools.partial(_sb_prompt_kernel, blk=blk),
        grid=(b, N_KV_HEADS, nb),
        in_specs=[
            pl.BlockSpec((1, GROUP, blk, SB_AUG), lambda bi, g, i: (bi, g, i, 0)),
            pl.BlockSpec((1, 1, s, SB_AUG), lambda bi, g, i: (bi, g, 0, 0)),
            pl.BlockSpec((1, 1, HEAD_DIM, s), lambda bi, g, i: (bi, g, 0, 0)),
            pl.BlockSpec((blk, blk), lambda bi, g, i: (0, 0)),
        ],
        out_specs=pl.BlockSpec((1, blk, GROUP * HEAD_DIM), lambda bi, g, i: (bi, i, g)),
        out_shape=jax.ShapeDtypeStruct((b, s, N_HEADS * HEAD_DIM), F32),
        scratch_shapes=[pltpu.VMEM((GROUP, HEAD_DIM, blk), F32),
                        pltpu.VMEM((SUBLANES, GROUP * blk), F32),
                        pltpu.VMEM((blk, GROUP * blk), F32),
                        pltpu.VMEM((blk, GROUP * blk), F32)],
        compiler_params=_params("parallel", "parallel", "arbitrary"),
        name="sb_prompt",
    )(q_aug, k_aug, vt, mask)


SB_DECODE_PAGES = 64


def _flat_queries(q_aug):
    n = q_aug.shape[2]
    return q_aug[0, :, :, :HEAD_DIM].transpose(1, 0, 2).reshape(n, N_HEADS * HEAD_DIM)


def _block_diag_queries(q):
    n = q.shape[0]
    q4 = q.reshape(n, N_HEADS, 1, HEAD_DIM)
    sel = (np.arange(N_HEADS)[:, None] // GROUP == np.arange(N_KV_HEADS)[None, :])
    sel = jnp.asarray(sel.reshape(1, N_HEADS, N_KV_HEADS, 1), q.dtype)
    return (q4 * sel).reshape(n, N_HEADS, KV_WIDTH)


def _own_kv_head_columns(o):
    n = o.shape[0]
    o5 = o.reshape(n, N_KV_HEADS, GROUP, N_KV_HEADS, HEAD_DIM)
    own = jnp.stack([o5[:, g, :, g, :] for g in range(N_KV_HEADS)], axis=1)
    return own.reshape(n, N_HEADS * HEAD_DIM)


def _split3(x):
    hi = x.astype(BF16)
    r1 = x - hi.astype(F32)
    mid = r1.astype(BF16)
    lo = (r1 - mid.astype(F32)).astype(BF16)
    return hi, mid, lo


def _sb_decode_kernel(pt_ref, q_ref, bias_ref, tri_ref, *refs, pages):
    k_refs = refs[:pages]
    v_refs = refs[pages:2 * pages]
    o_ref, carry_ref, acc_ref = refs[2 * pages:]
    c = pl.program_id(1)

    @pl.when(c == 0)
    def _():
        carry_ref[...] = jnp.zeros_like(carry_ref)
        acc_ref[...] = jnp.zeros_like(acc_ref)

    qbd = q_ref[0]
    tri = tri_ref[...]
    dot = functools.partial(jnp.dot, preferred_element_type=F32)
    z = jnp.concatenate([dot(qbd, k_refs[j][0, 0].astype(BF16)) for j in range(pages)], axis=0)
    z = z + jnp.tile(bias_ref[...], (pages, 1))
    sp = _softplus(z)
    hi, mid, lo = _split3(sp)
    suffix = dot(hi, tri) + dot(mid, tri) + dot(lo, tri)
    logw = z - suffix
    totals = jnp.broadcast_to(suffix[:, 0:1], suffix.shape)
    carry = carry_ref[...]
    acc = acc_ref[...]
    for j in reversed(range(pages)):
        rows = slice(j * N_HEADS, (j + 1) * N_HEADS)
        a = jnp.exp(logw[rows] - carry)
        carry = carry + totals[rows]
        acc = acc + lax.dot_general(a.astype(BF16), v_refs[j][0, 0].astype(BF16), NT_DIMS,
                                    preferred_element_type=F32)
    carry_ref[...] = carry
    acc_ref[...] = acc

    @pl.when(c == pl.num_programs(1) - 1)
    def _():
        o_ref[0] = acc


def _sb_decode(q, cache_kt, cache_vt, layer, page_table, bias):
    n = q.shape[0]
    n_pages = page_table.shape[1]
    page = cache_kt.shape[-1]
    pages = SB_DECODE_PAGES if n_pages % SB_DECODE_PAGES == 0 else n_pages
    n_chunks = n_pages // pages
    qbd = _block_diag_queries(q)
    bias_b = jnp.broadcast_to(bias.astype(F32)[:, None], (N_HEADS, page))
    tri = jnp.asarray(np.tril(np.ones((page, page), np.float32)), BF16)

    def page_spec(j):
        def index(bi, c, pt):
            return (layer, pt[bi, (n_chunks - 1 - c) * pages + j], 0, 0)
        return pl.BlockSpec((1, 1, KV_WIDTH, page), index)

    grid_spec = pltpu.PrefetchScalarGridSpec(
        num_scalar_prefetch=1,
        grid=(n, n_chunks),
        in_specs=[pl.BlockSpec((1, N_HEADS, KV_WIDTH), lambda bi, c, pt: (bi, 0, 0)),
                  pl.BlockSpec((N_HEADS, page), lambda bi, c, pt: (0, 0)),
                  pl.BlockSpec((page, page), lambda bi, c, pt: (0, 0))]
                 + [page_spec(j) for j in range(pages)] * 2,
        out_specs=pl.BlockSpec((1, N_HEADS, KV_WIDTH), lambda bi, c, pt: (bi, 0, 0)),
        scratch_shapes=[pltpu.VMEM((N_HEADS, page), F32), pltpu.VMEM((N_HEADS, KV_WIDTH), F32)],
    )
    out = pl.pallas_call(
        functools.partial(_sb_decode_kernel, pages=pages),
        grid_spec=grid_spec,
        out_shape=jax.ShapeDtypeStruct((n, N_HEADS, KV_WIDTH), F32),
        compiler_params=_params("parallel", "arbitrary"),
        name="sb_decode",
    )(page_table, qbd, bias_b, tri, *([cache_kt] * pages), *([cache_vt] * pages))
    return _own_kv_head_columns(out)


SWA_Q_TILE = 2 * WINDOW


def _swa_prompt_kernel(sink_ref, q_ref, kp_ref, kc_ref, vp_ref, vc_ref, o_ref):
    g = pl.program_id(1)
    n = pl.program_id(2)
    blk = WINDOW
    sub_blocks = kc_ref.shape[2] // blk
    kband = jnp.concatenate([kp_ref[0, 0], kc_ref[0, 0]], axis=0)
    vband = jnp.concatenate([vp_ref[0, 0], vc_ref[0, 0]], axis=1)
    sk = lax.broadcasted_iota(jnp.int32, (2 * blk, blk), 0)
    t = lax.broadcasted_iota(jnp.int32, (2 * blk, blk), 1)
    diff = t + blk - sk
    in_window = (diff >= 0) & (diff <= WINDOW)
    sink = jnp.concatenate([jnp.full((1, blk), sink_ref[g * GROUP + r], F32) for r in range(GROUP)],
                           axis=1)
    for u in range(sub_blocks):
        q4 = q_ref[0, :, u * blk:(u + 1) * blk, :].reshape(GROUP * blk, q_ref.shape[3])
        kb = kband[u * blk:(u + 2) * blk]
        sc = lax.dot_general(kb, q4, NT_DIMS, preferred_element_type=F32)
        valid = in_window & ((sk >= blk) | (n * sub_blocks + u > 0))
        sc = jnp.where(jnp.tile(valid, (1, GROUP)), sc, MASK_VALUE)
        m = jnp.maximum(jnp.max(sc, axis=0, keepdims=True), sink)
        p = jnp.exp(sc - m)
        den = jnp.sum(p, axis=0, keepdims=True) + jnp.exp(sink - m)
        w = (p * (1.0 / den)).astype(BF16)
        ot = jnp.dot(vband[:, u * blk:(u + 2) * blk], w, preferred_element_type=F32)
        stacked = jnp.concatenate([ot[:, r * blk:(r + 1) * blk] for r in range(GROUP)], axis=0)
        o_ref[0, u * blk:(u + 1) * blk, :] = stacked.T


def _to_heads(x, b, s, heads):
    return x.reshape(b, s, heads, HEAD_DIM).transpose(0, 2, 1, 3)


def _swa_prompt(q_aug, kh, vt, sink, b, s):
    tile = SWA_Q_TILE if s % SWA_Q_TILE == 0 else WINDOW
    per_tile = tile // WINDOW
    prev = lambda bi, g, n: (bi, g, jnp.maximum(n * per_tile - 1, 0), 0)
    prev_t = lambda bi, g, n: (bi, g, 0, jnp.maximum(n * per_tile - 1, 0))
    out = pl.pallas_call(
        _swa_prompt_kernel,
        grid=(b, N_KV_HEADS, s // tile),
        in_specs=[pl.BlockSpec(memory_space=pltpu.SMEM),
                  pl.BlockSpec((1, GROUP, tile, SB_AUG), lambda bi, g, n: (bi, g, n, 0)),
                  pl.BlockSpec((1, 1, WINDOW, SB_AUG), prev),
                  pl.BlockSpec((1, 1, tile, SB_AUG), lambda bi, g, n: (bi, g, n, 0)),
                  pl.BlockSpec((1, 1, HEAD_DIM, WINDOW), prev_t),
                  pl.BlockSpec((1, 1, HEAD_DIM, tile), lambda bi, g, n: (bi, g, 0, n))],
        out_specs=pl.BlockSpec((1, tile, GROUP * HEAD_DIM), lambda bi, g, n: (bi, n, g)),
        out_shape=jax.ShapeDtypeStruct((b, s, N_HEADS * HEAD_DIM), F32),
        compiler_params=_params("parallel", "parallel", "parallel"),
        name="swa_prompt",
    )(sink, q_aug, kh, kh, vt, vt)
    return out.reshape(b * s, N_HEADS * HEAD_DIM)


SWA_DECODE_ROWS = 8


def _swa_decode_kernel(qbd_ref, q_ref, knew_ref, vnew_ref, sink_ref, kt_ref, vt_ref, o_ref):
    sink = sink_ref[...]
    for i in range(qbd_ref.shape[0]):
        z = jnp.dot(qbd_ref[i], kt_ref[i].astype(BF16), preferred_element_type=F32)
        zn = jnp.sum(q_ref[i].astype(F32) * knew_ref[i], axis=-1, keepdims=True)
        m = jnp.maximum(jnp.maximum(jnp.max(z, axis=-1, keepdims=True), zn), sink)
        p = jnp.exp(z - m)
        pn = jnp.exp(zn - m)
        inv = 1.0 / (jnp.sum(p, axis=-1, keepdims=True) + pn + jnp.exp(sink - m))
        out = lax.dot_general((p * inv).astype(BF16), vt_ref[i].astype(BF16), NT_DIMS,
                              preferred_element_type=F32)
        o_ref[i] = out + (pn * inv) * vnew_ref[i]


def _swa_decode(q, k_new, v_new, state_kt, state_vt, sink):
    n = q.shape[0]
    buf = state_kt.shape[-1]
    qbd = _block_diag_queries(q)
    q16 = q.reshape(n, N_HEADS, HEAD_DIM)
    knew16 = jnp.repeat(k_new.reshape(n, N_KV_HEADS, HEAD_DIM), GROUP, axis=1)
    per_n = lambda i: (i, 0, 0)
    tn = SWA_DECODE_ROWS if n % SWA_DECODE_ROWS == 0 else 1
    out = pl.pallas_call(
        _swa_decode_kernel,
        grid=(n // tn,),
        in_specs=[pl.BlockSpec((tn, N_HEADS, KV_WIDTH), per_n),
                  pl.BlockSpec((tn, N_HEADS, HEAD_DIM), per_n),
                  pl.BlockSpec((tn, N_HEADS, HEAD_DIM), per_n),
                  pl.BlockSpec((tn, 1, KV_WIDTH), per_n),
                  pl.BlockSpec((N_HEADS, 1), lambda i: (0, 0)),
                  pl.BlockSpec((tn, KV_WIDTH, buf), per_n),
                  pl.BlockSpec((tn, KV_WIDTH, buf), per_n)],
        out_specs=pl.BlockSpec((tn, N_HEADS, KV_WIDTH), per_n),
        out_shape=jax.ShapeDtypeStruct((n, N_HEADS, KV_WIDTH), F32),
        compiler_params=_params("parallel"),
        name="swa_decode",
    )(qbd, q16, knew16, v_new.reshape(n, 1, KV_WIDTH), sink.astype(F32).reshape(N_HEADS, 1),
      state_kt, state_vt)
    return _own_kv_head_columns(out)


def _rope_tables(pos):
    half = HEAD_DIM // 2
    inv = ROPE_THETA ** (-jnp.arange(half, dtype=F32) * 2.0 / HEAD_DIM)
    ang = pos.astype(F32)[:, None] * inv[None, :]
    reps = LANES // half
    return jnp.tile(jnp.cos(ang), (1, reps)), jnp.tile(jnp.sin(ang), (1, reps))


def _with_rotate_half(w, rest):
    d = w.shape[0]
    w4 = w.reshape(d, -1, 2, HEAD_DIM // 2)
    rot = jnp.concatenate([-w4[:, :, 1:2], w4[:, :, 0:1]], axis=2).reshape(d, -1)
    return jnp.concatenate([w, rot, rest], axis=1).astype(BF16)


def kernel(x_prompt, x_sample, cache_a_k, cache_a_v, state_b_k, state_b_v, page_table, w_in_a,
           sb_bias, w_in_b, w_kv_shared, w_out, sinks, ln_g, ln_b):
    b, s, d = x_prompt.shape
    n, dec_t, _ = x_sample.shape
    assert dec_t == 1 and d == N_HEADS * HEAD_DIM
    n_a = w_in_a.shape[0]
    n_b = w_in_b.shape[0]
    depth = w_out.shape[0]
    alpha = (2.0 * depth) ** 0.25
    page = cache_a_k.shape[2]
    past = page_table.shape[1] * page
    buf = state_b_k.shape[1]
    assert buf == WINDOW and s % WINDOW == 0

    xp = x_prompt.reshape(b * s, d)
    xs = x_sample.reshape(n, d)
    w_out_bf = w_out.astype(BF16)
    cache_kt = cache_a_k.transpose(0, 1, 3, 4, 2).reshape(n_a, -1, KV_WIDTH, page)
    cache_vt = cache_a_v.transpose(0, 1, 3, 4, 2).reshape(n_a, -1, KV_WIDTH, page)

    ak_p, av_p, ak_s, av_s = [], [], [], []
    for l in range(n_a):
        wa = w_in_a[l].astype(BF16)
        aug = _sb_aug_rows(sb_bias[l])
        qp, kp, vp, gp = _proj_a(xp, wa, aug, b)
        qs, ks, vs, gs = _proj_a(xs, wa, aug, 1)
        op = _sb_prompt(qp, kp, vp, b, s).reshape(b * s, d)
        os_ = _sb_decode(_flat_queries(qs), cache_kt, cache_vt, l, page_table, sb_bias[l])
        xp = _post(op, gp, xp, w_out_bf[l], ln_g[l], ln_b[l], alpha)
        xs = _post(os_, gs, xs, w_out_bf[l], ln_g[l], ln_b[l], alpha)
        ak_p.append(kp.reshape(b, s, N_KV_HEADS, HEAD_DIM))
        av_p.append(vp.reshape(b, s, N_KV_HEADS, HEAD_DIM))
        ak_s.append(ks.reshape(n, 1, N_KV_HEADS, HEAD_DIM))
        av_s.append(vs.reshape(n, 1, N_KV_HEADS, HEAD_DIM))

    cos_p, sin_p = _rope_tables(jnp.arange(s))
    cos_p, sin_p = jnp.tile(cos_p, (b, 1)), jnp.tile(sin_p, (b, 1))
    cos_s, sin_s = _rope_tables(jnp.full((n,), past))

    w_kv = _with_rotate_half(w_kv_shared[:, :KV_WIDTH], w_kv_shared[:, KV_WIDTH:])
    kp_sh, vp_sh = _proj_kv(xp, w_kv, cos_p, sin_p)
    ks_sh, vs_sh = _proj_kv(xs, w_kv, cos_s, sin_s)
    kh = jnp.pad(_to_heads(kp_sh.astype(BF16), b, s, N_KV_HEADS),
                 ((0, 0), (0, 0), (0, 0), (0, SB_AUG - HEAD_DIM)))
    vt = vp_sh.astype(BF16).reshape(b, s, N_KV_HEADS, HEAD_DIM).transpose(0, 2, 3, 1)
    no_aug = jnp.zeros((N_HEADS, SB_AUG), F32)
    state_kt = state_b_k.transpose(0, 2, 3, 1).reshape(n, KV_WIDTH, buf)
    state_vt = state_b_v.transpose(0, 2, 3, 1).reshape(n, KV_WIDTH, buf)

    for j in range(n_b):
        l = n_a + j
        wb = _with_rotate_half(w_in_b[j][:, :d], w_in_b[j][:, d:])
        qp, gp = _proj_b(xp, wb, cos_p, sin_p, no_aug, b)
        qs, gs = _proj_b(xs, wb, cos_s, sin_s, no_aug, 1)
        op = _swa_prompt(qp, kh, vt, sinks[j].astype(F32), b, s)
        os_ = _swa_decode(_flat_queries(qs), ks_sh, vs_sh, state_kt, state_vt, sinks[j])
        xp = _post(op, gp, xp, w_out_bf[l], ln_g[l], ln_b[l], alpha)
        xs = _post(os_, gs, xs, w_out_bf[l], ln_g[l], ln_b[l], alpha)

    kp4 = kp_sh.reshape(b, s, N_KV_HEADS, HEAD_DIM)
    vp4 = vp_sh.reshape(b, s, N_KV_HEADS, HEAD_DIM)
    ks4 = ks_sh.reshape(n, 1, N_KV_HEADS, HEAD_DIM)
    vs4 = vs_sh.reshape(n, 1, N_KV_HEADS, HEAD_DIM)
    w_p = min(WINDOW, s)
    return (xp.reshape(b, s, d), xs.reshape(n, 1, d),
            jnp.stack(ak_p), jnp.stack(av_p), jnp.stack(ak_s), jnp.stack(av_s),
            kp4[:, s - w_p:], vp4[:, s - w_p:],
            jnp.concatenate([state_b_k, ks4], axis=1)[:, dec_t:],
            jnp.concatenate([state_b_v, vs4], axis=1)[:, dec_t:])
```

```python
import functools

import numpy as np
import jax
import jax.numpy as jnp
from jax import lax
from jax.experimental import pallas as pl
from jax.experimental.pallas import tpu as pltpu

HEAD_DIM = 64
N_HEADS = 16
N_KV_HEADS = 4
GROUP = N_HEADS // N_KV_HEADS
KV_WIDTH = N_KV_HEADS * HEAD_DIM
WINDOW = 128
ROPE_THETA = 10000.0
LN_EPS = 1e-5
MASK_VALUE = -1e30
QK_SCALE = HEAD_DIM ** -0.5
LOG2E = 1.4426950408889634
SB_AUG = 2 * HEAD_DIM

SUBLANES = 8
LANES = 128
VMEM_LIMIT_BYTES = 48 * 1024 * 1024

BF16 = jnp.bfloat16
F32 = jnp.float32
NT_DIMS = (((1,), (1,)), ((), ()))


def _params(*semantics):
    return pltpu.CompilerParams(dimension_semantics=semantics,
                                vmem_limit_bytes=VMEM_LIMIT_BYTES)


def _row_tile(m):
    for t in (512, 256, 128):
        if m % t == 0:
            return t
    return m


def _softplus(z):
    return jnp.maximum(z, 0.0) + jnp.log(1.0 + jnp.exp(-jnp.abs(z)))


def _store_heads(q_ref, q, aug_ref):
    rows = q.shape[0]
    left = lax.broadcasted_iota(jnp.int32, (rows, LANES), 1) < HEAD_DIM
    for pair in range(N_HEADS // 2):
        slab = q[:, pair * LANES:(pair + 1) * LANES]
        even, odd = 2 * pair, 2 * pair + 1
        q_ref[0, even] = jnp.where(left, slab, aug_ref[even:even + 1, :]).astype(BF16)
        q_ref[0, odd] = jnp.where(left, pltpu.roll(slab, HEAD_DIM, 1),
                                  aug_ref[odd:odd + 1, :]).astype(BF16)


def _head_major_spec(tm, tiles_per_batch):
    return pl.BlockSpec((1, N_HEADS, tm, SB_AUG),
                        lambda i: (i // tiles_per_batch, 0, i % tiles_per_batch, 0))


def _proj_a_kernel(x_ref, w_ref, aug_ref, q_ref, k_ref, v_ref, g_ref):
    xb = x_ref[...].astype(BF16)
    d = g_ref.shape[1]
    kw = k_ref.shape[1]
    dot = functools.partial(jnp.dot, preferred_element_type=F32)
    _store_heads(q_ref, dot(xb, w_ref[:, 0:d]) * QK_SCALE, aug_ref)
    k_ref[...] = dot(xb, w_ref[:, d:d + kw])
    v_ref[...] = dot(xb, w_ref[:, d + kw:d + 2 * kw])
    g_ref[...] = dot(xb, w_ref[:, d + 2 * kw:])


def _proj_a(x, w, aug, b):
    m, d = x.shape
    n = w.shape[1]
    s = m // b
    tm = _row_tile(s)
    row = lambda i: (i, 0)
    const = lambda i: (0, 0)
    return pl.pallas_call(
        _proj_a_kernel,
        grid=(m // tm,),
        in_specs=[pl.BlockSpec((tm, d), row), pl.BlockSpec((d, n), const),
                  pl.BlockSpec((N_HEADS, SB_AUG), const)],
        out_specs=[_head_major_spec(tm, s // tm), pl.BlockSpec((tm, KV_WIDTH), row),
                   pl.BlockSpec((tm, KV_WIDTH), row), pl.BlockSpec((tm, d), row)],
        out_shape=[jax.ShapeDtypeStruct((b, N_HEADS, s, SB_AUG), BF16),
                   jax.ShapeDtypeStruct((m, KV_WIDTH), F32),
                   jax.ShapeDtypeStruct((m, KV_WIDTH), F32), jax.ShapeDtypeStruct((m, d), F32)],
        compiler_params=_params("parallel"),
        name="proj_a",
    )(x, w, aug)


def _proj_b_kernel(x_ref, w_ref, cos_ref, sin_ref, aug_ref, q_ref, g_ref):
    xb = x_ref[...].astype(BF16)
    d = g_ref.shape[1]
    dot = functools.partial(jnp.dot, preferred_element_type=F32)
    reps = d // cos_ref.shape[1]
    cos = jnp.tile(cos_ref[...], (1, reps))
    sin = jnp.tile(sin_ref[...], (1, reps))
    q = dot(xb, w_ref[:, 0:d]) * cos + dot(xb, w_ref[:, d:2 * d]) * sin
    _store_heads(q_ref, q * QK_SCALE, aug_ref)
    g_ref[...] = dot(xb, w_ref[:, 2 * d:])


def _proj_b(x, w, cos, sin, aug, b):
    m, d = x.shape
    n = w.shape[1]
    s = m // b
    tm = _row_tile(s)
    row = lambda i: (i, 0)
    const = lambda i: (0, 0)
    return pl.pallas_call(
        _proj_b_kernel,
        grid=(m // tm,),
        in_specs=[pl.BlockSpec((tm, d), row), pl.BlockSpec((d, n), const),
                  pl.BlockSpec((tm, LANES), row), pl.BlockSpec((tm, LANES), row),
                  pl.BlockSpec((N_HEADS, SB_AUG), const)],
        out_specs=[_head_major_spec(tm, s // tm), pl.BlockSpec((tm, d), row)],
        out_shape=[jax.ShapeDtypeStruct((b, N_HEADS, s, SB_AUG), BF16),
                   jax.ShapeDtypeStruct((m, d), F32)],
        compiler_params=_params("parallel"),
        name="proj_b",
    )(x, w, cos, sin, aug)


def _proj_kv_kernel(x_ref, w_ref, cos_ref, sin_ref, k_ref, v_ref):
    xb = x_ref[...].astype(BF16)
    kw = k_ref.shape[1]
    dot = functools.partial(jnp.dot, preferred_element_type=F32)
    reps = kw // cos_ref.shape[1]
    cos = jnp.tile(cos_ref[...], (1, reps))
    sin = jnp.tile(sin_ref[...], (1, reps))
    k_ref[...] = dot(xb, w_ref[:, 0:kw]) * cos + dot(xb, w_ref[:, kw:2 * kw]) * sin
    v_ref[...] = dot(xb, w_ref[:, 2 * kw:])


def _proj_kv(x, w, cos, sin):
    m, d = x.shape
    n = w.shape[1]
    tm = _row_tile(m)
    row = lambda i: (i, 0)
    return pl.pallas_call(
        _proj_kv_kernel,
        grid=(m // tm,),
        in_specs=[pl.BlockSpec((tm, d), row), pl.BlockSpec((d, n), lambda i: (0, 0)),
                  pl.BlockSpec((tm, LANES), row), pl.BlockSpec((tm, LANES), row)],
        out_specs=[pl.BlockSpec((tm, KV_WIDTH), row), pl.BlockSpec((tm, KV_WIDTH), row)],
        out_shape=[jax.ShapeDtypeStruct((m, KV_WIDTH), F32), jax.ShapeDtypeStruct((m, KV_WIDTH), F32)],
        compiler_params=_params("parallel"),
        name="proj_kv",
    )(x, w, cos, sin)


def _post_kernel(o_ref, g_ref, x_ref, w_ref, lng_ref, lnb_ref, out_ref, *, alpha):
    gate = g_ref[...]
    mixed = (o_ref[...] * (gate * jax.nn.sigmoid(gate))).astype(BF16)
    h = alpha * x_ref[...] + jnp.dot(mixed, w_ref[...], preferred_element_type=F32)
    mu = jnp.mean(h, axis=-1, keepdims=True)
    c = h - mu
    var = jnp.mean(c * c, axis=-1, keepdims=True)
    out_ref[...] = c * lax.rsqrt(var + LN_EPS) * lng_ref[...] + lnb_ref[...]


def _post(o, gate, x, w, ln_g, ln_b, alpha):
    m, d = x.shape
    tm = _row_tile(m)
    row = lambda i: (i, 0)
    const = lambda i: (0, 0)
    return pl.pallas_call(
        functools.partial(_post_kernel, alpha=alpha),
        grid=(m // tm,),
        in_specs=[pl.BlockSpec((tm, d), row), pl.BlockSpec((tm, d), row), pl.BlockSpec((tm, d), row),
                  pl.BlockSpec((d, d), const), pl.BlockSpec((1, d), const), pl.BlockSpec((1, d), const)],
        out_specs=pl.BlockSpec((tm, d), row),
        out_shape=jax.ShapeDtypeStruct((m, d), F32),
        compiler_params=_params("parallel"),
        name="post",
    )(o, gate, x, w, ln_g.reshape(1, d), ln_b.reshape(1, d))


SB_BLOCK = 256
SB_UNROLL = 4


def _sb_prompt_kernel(q_ref, k_ref, vt_ref, mask_ref, o_ref, acc_ref, carry_ref, z_ref, p_ref, *,
                      blk):
    i = pl.program_id(2)
    n_groups = blk // SUBLANES
    width = GROUP * blk
    acc_ref[...] = jnp.zeros_like(acc_ref)
    carry_ref[...] = jnp.zeros_like(carry_ref)
    sub = lax.broadcasted_iota(jnp.int32, (SUBLANES, width), 0)
    q4 = q_ref[0].reshape(width, SB_AUG)

    def logits(jj):
        start = pl.multiple_of(jj * blk, blk)
        kb = k_ref[0, 0, pl.ds(start, blk), :]
        z_ref[...] = lax.dot_general(kb, q4, NT_DIMS, preferred_element_type=F32)

    def kv_block(jj, masked):
        start = pl.multiple_of(jj * blk, blk)
        vb = vt_ref[0, 0, :, pl.ds(start, blk)]
        run = jnp.zeros((SUBLANES, width), F32)
        for t in reversed(range(n_groups)):
            rows = slice(t * SUBLANES, (t + 1) * SUBLANES)
            zt = z_ref[rows, :]
            if masked:
                zt = zt + jnp.tile(mask_ref[rows, :], (1, GROUP))
            e = jnp.exp2(jnp.abs(zt) * (-LOG2E))
            run = run + (jnp.maximum(zt, 0.0) + jnp.log(1.0 + e))
            p_ref[rows, :] = zt - run
        logits(jnp.maximum(jj - 1, 0))
        suffix = run
        for sh in (1, 2, 4):
            rolled = pltpu.roll(suffix, SUBLANES - sh, 0)
            suffix = suffix + jnp.where(sub < SUBLANES - sh, rolled, 0.0)
        carry = carry_ref[...]
        off = (carry + (suffix - run)) * LOG2E
        carry_ref[...] = carry + jnp.broadcast_to(suffix[0:1], (SUBLANES, width))
        for r in range(GROUP):
            cols = slice(r * blk, (r + 1) * blk)
            off_r = off[:, cols]
            a = jnp.concatenate(
                [jnp.exp2(p_ref[t * SUBLANES:(t + 1) * SUBLANES, cols] * LOG2E - off_r)
                 for t in range(n_groups)], axis=0).astype(BF16)
            acc_ref[r] += jnp.dot(vb, a, preferred_element_type=F32)

    logits(i)
    kv_block(i, True)

    def earlier_group(t, c):
        jj = i - 1 - SB_UNROLL * t
        for u in range(SB_UNROLL):
            kv_block(jj - u, False)
        return c

    groups = i // SB_UNROLL
    lax.fori_loop(0, groups, earlier_group, 0)

    def earlier_single(t, c):
        kv_block(i - 1 - SB_UNROLL * groups - t, False)
        return c

    lax.fori_loop(0, i % SB_UNROLL, earlier_single, 0)

    o_ref[0] = acc_ref[...].reshape(GROUP * HEAD_DIM, blk).T


def _sb_block_size(s):
    return SB_BLOCK if s % SB_BLOCK == 0 else LANES


def _sb_diag_mask(blk):
    rows = np.arange(blk)
    key = (rows % SUBLANES) * (blk // SUBLANES) + rows // SUBLANES
    return np.where(key[:, None] < np.arange(blk)[None, :], 0.0, MASK_VALUE).astype(np.float32)


def _sb_aug_rows(bias):
    bias3 = jnp.stack([p.astype(F32) for p in _split3(bias.astype(F32))], axis=-1)
    return jnp.pad(bias3, ((0, 0), (HEAD_DIM, SB_AUG - HEAD_DIM - 3)))


def _sb_prompt(q_aug, k, v, b, s):
    blk = _sb_block_size(s)
    nb = s // blk
    ng = blk // SUBLANES
    pad = SB_AUG - HEAD_DIM - 3
    k6 = k.astype(BF16).reshape(b, nb, SUBLANES, ng, N_KV_HEADS, HEAD_DIM)
    kperm = k6.transpose(0, 4, 1, 3, 2, 5).reshape(b, N_KV_HEADS, s, HEAD_DIM)
    ones_cols = jnp.pad(jnp.ones((3,), BF16), (0, pad))
    k_aug = jnp.concatenate(
        [kperm, jnp.broadcast_to(ones_cols, (b, N_KV_HEADS, s, SB_AUG - HEAD_DIM))], axis=-1)
    v6 = v.astype(BF16).reshape(b, nb, SUBLANES, ng, N_KV_HEADS, HEAD_DIM)
    vt = v6.transpose(0, 4, 5, 1, 3, 2).reshape(b, N_KV_HEADS, HEAD_DIM, s)
    mask = jnp.asarray(_sb_diag_mask(blk))
    return pl.pallas_call(
        functools.partial(_sb_prompt_kernel, blk=blk),
        grid=(b, N_KV_HEADS, nb),
        in_specs=[
            pl.BlockSpec((1, GROUP, blk, SB_AUG), lambda bi, g, i: (bi, g, i, 0)),
            pl.BlockSpec((1, 1, s, SB_AUG), lambda bi, g, i: (bi, g, 0, 0)),
            pl.BlockSpec((1, 1, HEAD_DIM, s), lambda bi, g, i: (bi, g, 0, 0)),
            pl.BlockSpec((blk, blk), lambda bi, g, i: (0, 0)),
        ],
        out_specs=pl.BlockSpec((1, blk, GROUP * HEAD_DIM), lambda bi, g, i: (bi, i, g)),
        out_shape=jax.ShapeDtypeStruct((b, s, N_HEADS * HEAD_DIM), F32),
        scratch_shapes=[pltpu.VMEM((GROUP, HEAD_DIM, blk), F32),
                        pltpu.VMEM((SUBLANES, GROUP * blk), F32),
                        pltpu.VMEM((blk, GROUP * blk), F32),
                        pltpu.VMEM((blk, GROUP * blk), F32)],
        compiler_params=_params("parallel", "parallel", "arbitrary"),
        name="sb_prompt",
    )(q_aug, k_aug, vt, mask)


SB_DECODE_PAGES = 64


def _flat_queries(q_aug):
    n = q_aug.shape[2]
    return q_aug[0, :, :, :HEAD_DIM].transpose(1, 0, 2).reshape(n, N_HEADS * HEAD_DIM)


def _block_diag_queries(q):
    n = q.shape[0]
    q4 = q.reshape(n, N_HEADS, 1, HEAD_DIM)
    sel = (np.arange(N_HEADS)[:, None] // GROUP == np.arange(N_KV_HEADS)[None, :])
    sel = jnp.asarray(sel.reshape(1, N_HEADS, N_KV_HEADS, 1), q.dtype)
    return (q4 * sel).reshape(n, N_HEADS, KV_WIDTH)


def _own_kv_head_columns(o):
    n = o.shape[0]
    o5 = o.reshape(n, N_KV_HEADS, GROUP, N_KV_HEADS, HEAD_DIM)
    own = jnp.stack([o5[:, g, :, g, :] for g in range(N_KV_HEADS)], axis=1)
    return own.reshape(n, N_HEADS * HEAD_DIM)


def _split3(x):
    hi = x.astype(BF16)
    r1 = x - hi.astype(F32)
    mid = r1.astype(BF16)
    lo = (r1 - mid.astype(F32)).astype(BF16)
    return hi, mid, lo


def _sb_decode_kernel(pt_ref, q_ref, bias_ref, tri_ref, *refs, pages):
    k_refs = refs[:pages]
    v_refs = refs[pages:2 * pages]
    o_ref, carry_ref, acc_ref = refs[2 * pages:]
    c = pl.program_id(1)

    @pl.when(c == 0)
    def _():
        carry_ref[...] = jnp.zeros_like(carry_ref)
        acc_ref[...] = jnp.zeros_like(acc_ref)

    qbd = q_ref[0]
    tri = tri_ref[...]
    dot = functools.partial(jnp.dot, preferred_element_type=F32)
    z = jnp.concatenate([dot(qbd, k_refs[j][0, 0].astype(BF16)) for j in range(pages)], axis=0)
    z = z + jnp.tile(bias_ref[...], (pages, 1))
    sp = _softplus(z)
    hi, mid, lo = _split3(sp)
    suffix = dot(hi, tri) + dot(mid, tri) + dot(lo, tri)
    logw = z - suffix
    totals = jnp.broadcast_to(suffix[:, 0:1], suffix.shape)
    carry = carry_ref[...]
    acc = acc_ref[...]
    for j in reversed(range(pages)):
        rows = slice(j * N_HEADS, (j + 1) * N_HEADS)
        a = jnp.exp(logw[rows] - carry)
        carry = carry + totals[rows]
        acc = acc + lax.dot_general(a.astype(BF16), v_refs[j][0, 0].astype(BF16), NT_DIMS,
                                    preferred_element_type=F32)
    carry_ref[...] = carry
    acc_ref[...] = acc

    @pl.when(c == pl.num_programs(1) - 1)
    def _():
        o_ref[0] = acc


def _sb_decode(q, cache_kt, cache_vt, layer, page_table, bias):
    n = q.shape[0]
    n_pages = page_table.shape[1]
    page = cache_kt.shape[-1]
    pages = SB_DECODE_PAGES if n_pages % SB_DECODE_PAGES == 0 else n_pages
    n_chunks = n_pages // pages
    qbd = _block_diag_queries(q)
    bias_b = jnp.broadcast_to(bias.astype(F32)[:, None], (N_HEADS, page))
    tri = jnp.asarray(np.tril(np.ones((page, page), np.float32)), BF16)

    def page_spec(j):
        def index(bi, c, pt):
            return (layer, pt[bi, (n_chunks - 1 - c) * pages + j], 0, 0)
        return pl.BlockSpec((1, 1, KV_WIDTH, page), index)

    grid_spec = pltpu.PrefetchScalarGridSpec(
        num_scalar_prefetch=1,
        grid=(n, n_chunks),
        in_specs=[pl.BlockSpec((1, N_HEADS, KV_WIDTH), lambda bi, c, pt: (bi, 0, 0)),
                  pl.BlockSpec((N_HEADS, page), lambda bi, c, pt: (0, 0)),
                  pl.BlockSpec((page, page), lambda bi, c, pt: (0, 0))]
                 + [page_spec(j) for j in range(pages)] * 2,
        out_specs=pl.BlockSpec((1, N_HEADS, KV_WIDTH), lambda bi, c, pt: (bi, 0, 0)),
        scratch_shapes=[pltpu.VMEM((N_HEADS, page), F32), pltpu.VMEM((N_HEADS, KV_WIDTH), F32)],
    )
    out = pl.pallas_call(
        functools.partial(_sb_decode_kernel, pages=pages),
        grid_spec=grid_spec,
        out_shape=jax.ShapeDtypeStruct((n, N_HEADS, KV_WIDTH), F32),
        compiler_params=_params("parallel", "arbitrary"),
        name="sb_decode",
    )(page_table, qbd, bias_b, tri, *([cache_kt] * pages), *([cache_vt] * pages))
    return _own_kv_head_columns(out)


SWA_Q_TILE = 4 * WINDOW


def _swa_prompt_kernel(sink_ref, q_ref, kp_ref, kc_ref, vp_ref, vc_ref, o_ref):
    g = pl.program_id(1)
    n = pl.program_id(2)
    blk = WINDOW
    sub_blocks = kc_ref.shape[2] // blk
    kband = jnp.concatenate([kp_ref[0, 0], kc_ref[0, 0]], axis=0)
    vband = jnp.concatenate([vp_ref[0, 0], vc_ref[0, 0]], axis=1)
    sk = lax.broadcasted_iota(jnp.int32, (2 * blk, blk), 0)
    t = lax.broadcasted_iota(jnp.int32, (2 * blk, blk), 1)
    diff = t + blk - sk
    in_window = (diff >= 0) & (diff <= WINDOW)
    sink = jnp.concatenate([jnp.full((1, blk), sink_ref[g * GROUP + r], F32) for r in range(GROUP)],
                           axis=1)
    for u in range(sub_blocks):
        q4 = q_ref[0, :, u * blk:(u + 1) * blk, :].reshape(GROUP * blk, q_ref.shape[3])
        kb = kband[u * blk:(u + 2) * blk]
        sc = lax.dot_general(kb, q4, NT_DIMS, preferred_element_type=F32)
        valid = in_window & ((sk >= blk) | (n * sub_blocks + u > 0))
        sc = jnp.where(jnp.tile(valid, (1, GROUP)), sc, MASK_VALUE)
        m = jnp.maximum(jnp.max(sc, axis=0, keepdims=True), sink)
        p = jnp.exp(sc - m)
        den = jnp.sum(p, axis=0, keepdims=True) + jnp.exp(sink - m)
        w = (p * (1.0 / den)).astype(BF16)
        ot = jnp.dot(vband[:, u * blk:(u + 2) * blk], w, preferred_element_type=F32)
        stacked = jnp.concatenate([ot[:, r * blk:(r + 1) * blk] for r in range(GROUP)], axis=0)
        o_ref[0, u * blk:(u + 1) * blk, :] = stacked.T


def _to_heads(x, b, s, heads):
    return x.reshape(b, s, heads, HEAD_DIM).transpose(0, 2, 1, 3)


def _swa_prompt(q_aug, kh, vt, sink, b, s):
    tile = SWA_Q_TILE if s % SWA_Q_TILE == 0 else WINDOW
    per_tile = tile // WINDOW
    prev = lambda bi, g, n: (bi, g, jnp.maximum(n * per_tile - 1, 0), 0)
    prev_t = lambda bi, g, n: (bi, g, 0, jnp.maximum(n * per_tile - 1, 0))
    out = pl.pallas_call(
        _swa_prompt_kernel,
        grid=(b, N_KV_HEADS, s // tile),
        in_specs=[pl.BlockSpec(memory_space=pltpu.SMEM),
                  pl.BlockSpec((1, GROUP, tile, SB_AUG), lambda bi, g, n: (bi, g, n, 0)),
                  pl.BlockSpec((1, 1, WINDOW, SB_AUG), prev),
                  pl.BlockSpec((1, 1, tile, SB_AUG), lambda bi, g, n: (bi, g, n, 0)),
                  pl.BlockSpec((1, 1, HEAD_DIM, WINDOW), prev_t),
                  pl.BlockSpec((1, 1, HEAD_DIM, tile), lambda bi, g, n: (bi, g, 0, n))],
        out_specs=pl.BlockSpec((1, tile, GROUP * HEAD_DIM), lambda bi, g, n: (bi, n, g)),
        out_shape=jax.ShapeDtypeStruct((b, s, N_HEADS * HEAD_DIM), F32),
        compiler_params=_params("parallel", "parallel", "parallel"),
        name="swa_prompt",
    )(sink, q_aug, kh, kh, vt, vt)
    return out.reshape(b * s, N_HEADS * HEAD_DIM)


SWA_DECODE_ROWS = 8


def _swa_decode_kernel(qbd_ref, q_ref, knew_ref, vnew_ref, sink_ref, kt_ref, vt_ref, o_ref):
    sink = sink_ref[...]
    for i in range(qbd_ref.shape[0]):
        z = jnp.dot(qbd_ref[i], kt_ref[i].astype(BF16), preferred_element_type=F32)
        zn = jnp.sum(q_ref[i].astype(F32) * knew_ref[i], axis=-1, keepdims=True)
        m = jnp.maximum(jnp.maximum(jnp.max(z, axis=-1, keepdims=True), zn), sink)
        p = jnp.exp(z - m)
        pn = jnp.exp(zn - m)
        inv = 1.0 / (jnp.sum(p, axis=-1, keepdims=True) + pn + jnp.exp(sink - m))
        out = lax.dot_general((p * inv).astype(BF16), vt_ref[i].astype(BF16), NT_DIMS,
                              preferred_element_type=F32)
        o_ref[i] = out + (pn * inv) * vnew_ref[i]


def _swa_decode(q, k_new, v_new, state_kt, state_vt, sink):
    n = q.shape[0]
    buf = state_kt.shape[-1]
    qbd = _block_diag_queries(q)
    q16 = q.reshape(n, N_HEADS, HEAD_DIM)
    knew16 = jnp.repeat(k_new.reshape(n, N_KV_HEADS, HEAD_DIM), GROUP, axis=1)
    per_n = lambda i: (i, 0, 0)
    tn = SWA_DECODE_ROWS if n % SWA_DECODE_ROWS == 0 else 1
    out = pl.pallas_call(
        _swa_decode_kernel,
        grid=(n // tn,),
        in_specs=[pl.BlockSpec((tn, N_HEADS, KV_WIDTH), per_n),
                  pl.BlockSpec((tn, N_HEADS, HEAD_DIM), per_n),
                  pl.BlockSpec((tn, N_HEADS, HEAD_DIM), per_n),
                  pl.BlockSpec((tn, 1, KV_WIDTH), per_n),
                  pl.BlockSpec((N_HEADS, 1), lambda i: (0, 0)),
                  pl.BlockSpec((tn, KV_WIDTH, buf), per_n),
                  pl.BlockSpec((tn, KV_WIDTH, buf), per_n)],
        out_specs=pl.BlockSpec((tn, N_HEADS, KV_WIDTH), per_n),
        out_shape=jax.ShapeDtypeStruct((n, N_HEADS, KV_WIDTH), F32),
        compiler_params=_params("parallel"),
        name="swa_decode",
    )(qbd, q16, knew16, v_new.reshape(n, 1, KV_WIDTH), sink.astype(F32).reshape(N_HEADS, 1),
      state_kt, state_vt)
    return _own_kv_head_columns(out)


def _rope_tables(pos):
    half = HEAD_DIM // 2
    inv = ROPE_THETA ** (-jnp.arange(half, dtype=F32) * 2.0 / HEAD_DIM)
    ang = pos.astype(F32)[:, None] * inv[None, :]
    reps = LANES // half
    return jnp.tile(jnp.cos(ang), (1, reps)), jnp.tile(jnp.sin(ang), (1, reps))


def _with_rotate_half(w, rest):
    d = w.shape[0]
    w4 = w.reshape(d, -1, 2, HEAD_DIM // 2)
    rot = jnp.concatenate([-w4[:, :, 1:2], w4[:, :, 0:1]], axis=2).reshape(d, -1)
    return jnp.concatenate([w, rot, rest], axis=1).astype(BF16)


def kernel(x_prompt, x_sample, cache_a_k, cache_a_v, state_b_k, state_b_v, page_table, w_in_a,
           sb_bias, w_in_b, w_kv_shared, w_out, sinks, ln_g, ln_b):
    b, s, d = x_prompt.shape
    n, dec_t, _ = x_sample.shape
    assert dec_t == 1 and d == N_HEADS * HEAD_DIM
    n_a = w_in_a.shape[0]
    n_b = w_in_b.shape[0]
    depth = w_out.shape[0]
    alpha = (2.0 * depth) ** 0.25
    page = cache_a_k.shape[2]
    past = page_table.shape[1] * page
    buf = state_b_k.shape[1]
    assert buf == WINDOW and s % WINDOW == 0

    xp = x_prompt.reshape(b * s, d)
    xs = x_sample.reshape(n, d)
    w_out_bf = w_out.astype(BF16)
    cache_kt = cache_a_k.transpose(0, 1, 3, 4, 2).reshape(n_a, -1, KV_WIDTH, page)
    cache_vt = cache_a_v.transpose(0, 1, 3, 4, 2).reshape(n_a, -1, KV_WIDTH, page)

    ak_p, av_p, ak_s, av_s = [], [], [], []
    for l in range(n_a):
        wa = w_in_a[l].astype(BF16)
        aug = _sb_aug_rows(sb_bias[l])
        qp, kp, vp, gp = _proj_a(xp, wa, aug, b)
        qs, ks, vs, gs = _proj_a(xs, wa, aug, 1)
        op = _sb_prompt(qp, kp, vp, b, s).reshape(b * s, d)
        os_ = _sb_decode(_flat_queries(qs), cache_kt, cache_vt, l, page_table, sb_bias[l])
        xp = _post(op, gp, xp, w_out_bf[l], ln_g[l], ln_b[l], alpha)
        xs = _post(os_, gs, xs, w_out_bf[l], ln_g[l], ln_b[l], alpha)
        ak_p.append(kp.reshape(b, s, N_KV_HEADS, HEAD_DIM))
        av_p.append(vp.reshape(b, s, N_KV_HEADS, HEAD_DIM))
        ak_s.append(ks.reshape(n, 1, N_KV_HEADS, HEAD_DIM))
        av_s.append(vs.reshape(n, 1, N_KV_HEADS, HEAD_DIM))

    cos_p, sin_p = _rope_tables(jnp.arange(s))
    cos_p, sin_p = jnp.tile(cos_p, (b, 1)), jnp.tile(sin_p, (b, 1))
    cos_s, sin_s = _rope_tables(jnp.full((n,), past))

    w_kv = _with_rotate_half(w_kv_shared[:, :KV_WIDTH], w_kv_shared[:, KV_WIDTH:])
    kp_sh, vp_sh = _proj_kv(xp, w_kv, cos_p, sin_p)
    ks_sh, vs_sh = _proj_kv(xs, w_kv, cos_s, sin_s)
    kh = jnp.pad(_to_heads(kp_sh.astype(BF16), b, s, N_KV_HEADS),
                 ((0, 0), (0, 0), (0, 0), (0, SB_AUG - HEAD_DIM)))
    vt = vp_sh.astype(BF16).reshape(b, s, N_KV_HEADS, HEAD_DIM).transpose(0, 2, 3, 1)
    no_aug = jnp.zeros((N_HEADS, SB_AUG), F32)
    state_kt = state_b_k.transpose(0, 2, 3, 1).reshape(n, KV_WIDTH, buf)
    state_vt = state_b_v.transpose(0, 2, 3, 1).reshape(n, KV_WIDTH, buf)

    for j in range(n_b):
        l = n_a + j
        wb = _with_rotate_half(w_in_b[j][:, :d], w_in_b[j][:, d:])
        qp, gp = _proj_b(xp, wb, cos_p, sin_p, no_aug, b)
        qs, gs = _proj_b(xs, wb, cos_s, sin_s, no_aug, 1)
        op = _swa_prompt(qp, kh, vt, sinks[j].astype(F32), b, s)
        os_ = _swa_decode(_flat_queries(qs), ks_sh, vs_sh, state_kt, state_vt, sinks[j])
        xp = _post(op, gp, xp, w_out_bf[l], ln_g[l], ln_b[l], alpha)
        xs = _post(os_, gs, xs, w_out_bf[l], ln_g[l], ln_b[l], alpha)

    kp4 = kp_sh.reshape(b, s, N_KV_HEADS, HEAD_DIM)
    vp4 = vp_sh.reshape(b, s, N_KV_HEADS, HEAD_DIM)
    ks4 = ks_sh.reshape(n, 1, N_KV_HEADS, HEAD_DIM)
    vs4 = vs_sh.reshape(n, 1, N_KV_HEADS, HEAD_DIM)
    w_p = min(WINDOW, s)
    return (xp.reshape(b, s, d), xs.reshape(n, 1, d),
            jnp.stack(ak_p), jnp.stack(av_p), jnp.stack(ak_s), jnp.stack(av_s),
            kp4[:, s - w_p:], vp4[:, s - w_p:],
            jnp.concatenate([state_b_k, ks4], axis=1)[:, dec_t:],
            jnp.concatenate([state_b_v, vs4], axis=1)[:, dec_t:])
```
